```python
import functools
import jax, jax.numpy as jnp
from jax import lax
import numpy as np

D_MODEL = 1024
BATCH = 2
SEQ = 8192
DEPTH = 1
DEC_BATCH = 32
DEC_SEQ = 1
PAST_LEN = 8192
PAGE_SIZE = 128

N_META = 16
N_HEADS = 8
HEAD_DIM = 64
ATTN_DIM = N_HEADS * HEAD_DIM
IDX_HEADS = 4
IDX_DIM = 64
IDX_SCALE = (IDX_HEADS * IDX_DIM) ** -0.5
TOPK_MAX = 256
CONV_DIM = 512
CONV_WIDTH = 3
D_FF = 2816
FFN_CONV_WIDTH = 3
ROPE_THETA = 10000.0
EPS = 1e-6
Q_BLOCK = 128
IN_SIZES = (ATTN_DIM, ATTN_DIM, ATTN_DIM, IDX_HEADS * IDX_DIM, IDX_DIM, IDX_HEADS,
            CONV_DIM, CONV_DIM, CONV_DIM, D_MODEL, D_MODEL)
IN_COLS = sum(IN_SIZES)

kernel_name = 'hybrid_dsa_shortconv_decode_step'


def rms_norm(x, w):
    xf = x.astype(jnp.float32)
    y = xf * lax.rsqrt(jnp.mean(xf * xf, axis=-1, keepdims=True) + EPS)
    return (y * w.astype(jnp.float32)).astype(x.dtype)


def rope(x, pos):
    d = x.shape[-1]
    half = d // 2
    freqs = ROPE_THETA ** (-jnp.arange(half, dtype=jnp.float32) / half)
    ang = pos.astype(jnp.float32)[:, None] * freqs[None, :]
    cos = jnp.cos(ang)[None, :, None, :]
    sin = jnp.sin(ang)[None, :, None, :]
    xf = x.astype(jnp.float32)
    x1, x2 = xf[..., :half], xf[..., half:]
    return jnp.concatenate([x1 * cos - x2 * sin, x1 * sin + x2 * cos], axis=-1).astype(x.dtype)


def causal_dwconv(u_hist, w):
    width = w.shape[0]
    n = u_hist.shape[1] - (width - 1)
    out = u_hist[:, 0:n] * w[0]
    for j in range(1, width):
        out = out + u_hist[:, j:j + n] * w[j]
    return out


def indexer_scores(qi, wi, ki):
    s = jnp.einsum('bqhd,bsd->bqhs', qi.astype(jnp.float32), ki.astype(jnp.float32))
    return jnp.einsum('bqhs,bqh->bqs', jax.nn.relu(s), wi.astype(jnp.float32)) * IDX_SCALE


def sparse_softmax(q, kg, vg, valid):
    logits = jnp.einsum('bqhd,bqkhd->bqhk', q.astype(jnp.float32), kg.astype(jnp.float32)) * (HEAD_DIM ** -0.5)
    logits = jnp.where(valid[:, :, None, :], logits, -jnp.inf)
    p = jax.nn.softmax(logits, axis=-1)
    return jnp.einsum('bqhk,bqkhd->bqhd', p, vg.astype(jnp.float32)).astype(q.dtype)


def prompt_sparse_attention(q, k, v, qi, wi, ki, topk):
    B, T = q.shape[0], q.shape[1]
    n_blk = -(-T // Q_BLOCK)
    t_pad = n_blk * Q_BLOCK

    def to_blocks(a):
        a = jnp.pad(a, [(0, 0), (0, t_pad - T)] + [(0, 0)] * (a.ndim - 2))
        return jnp.moveaxis(a.reshape((B, n_blk, Q_BLOCK) + a.shape[2:]), 1, 0)

    key_pos = jnp.arange(T, dtype=jnp.int32)

    def body(args):
        qb, qib, wib, start = args
        qpos = start + jnp.arange(Q_BLOCK, dtype=jnp.int32)
        sc = indexer_scores(qib, wib, ki)
        causal = key_pos[None, :] <= qpos[:, None]
        sc = jnp.where(causal[None], sc, -jnp.inf)
        _, sel = lax.top_k(sc, topk)
        valid = sel <= qpos[None, :, None]
        kg = jax.vmap(lambda kb, ib: kb[ib])(k, sel)
        vg = jax.vmap(lambda vb, ib: vb[ib])(v, sel)
        return sparse_softmax(qb, kg, vg, valid)

    starts = jnp.arange(n_blk, dtype=jnp.int32) * Q_BLOCK
    out = lax.map(body, (to_blocks(q), to_blocks(qi), to_blocks(wi), starts))
    return jnp.moveaxis(out, 0, 1).reshape(B, t_pad, N_HEADS, HEAD_DIM)[:, :T]


def sample_sparse_attention(q, k_new, v_new, qi, wi, ki_new, cache_k, cache_v, cache_kidx, page_table, topk):
    DB, S = q.shape[0], q.shape[1]
    past = page_table.shape[1] * PAGE_SIZE
    L = past + S
    ki_past = cache_kidx[page_table].reshape(DB, past, IDX_DIM)
    ki_all = jnp.concatenate([ki_past.astype(ki_new.dtype), ki_new], axis=1)
    qpos = past + jnp.arange(S, dtype=jnp.int32)
    sc = indexer_scores(qi, wi, ki_all)
    causal = jnp.arange(L, dtype=jnp.int32)[None, :] <= qpos[:, None]
    sc = jnp.where(causal[None], sc, -jnp.inf)
    _, sel = lax.top_k(sc, topk)
    valid = sel <= qpos[None, :, None]
    in_past = (sel < past)[..., None, None]
    ps = jnp.minimum(sel, past - 1)
    phys = jax.vmap(lambda pt, i: pt[i])(page_table, ps // PAGE_SIZE)
    off = ps % PAGE_SIZE
    ns = jnp.clip(sel - past, 0, S - 1)
    kg = jnp.where(in_past, cache_k[phys, off].astype(k_new.dtype), jax.vmap(lambda kb, ib: kb[ib])(k_new, ns))
    vg = jnp.where(in_past, cache_v[phys, off].astype(v_new.dtype), jax.vmap(lambda vb, ib: vb[ib])(v_new, ns))
    return sparse_softmax(q, kg, vg, valid)


def trunk_layer(x, pos, conv_prev, ffn_prev, attend, attn_norm_w, w_in, q_norm_w, k_norm_w, conv_w,
                w_o_attn, w_o_conv, w_merge, ffn_norm_w, w_up, ffn_conv_w, ffn_conv_b, w_down):
    B, T = x.shape[0], x.shape[1]
    xn = rms_norm(x, attn_norm_w)
    split_at = np.cumsum(IN_SIZES)[:-1].tolist()
    q, k, v, qi, ki, wi, cx, cb, cc, ga, gc = jnp.split(xn @ w_in, split_at, axis=-1)
    q = rope(rms_norm(q.reshape(B, T, N_HEADS, HEAD_DIM), q_norm_w), pos)
    k = rope(rms_norm(k.reshape(B, T, N_HEADS, HEAD_DIM), k_norm_w), pos)
    v = v.reshape(B, T, N_HEADS, HEAD_DIM)
    qi = rope(qi.reshape(B, T, IDX_HEADS, IDX_DIM), pos)
    ki = rope(ki[:, :, None, :], pos)[:, :, 0]
    attn = attend(q, k, v, qi, wi, ki).reshape(B, T, ATTN_DIM)
    u_hist = jnp.concatenate([conv_prev.astype(x.dtype), cc * cx], axis=1)
    conv_out = cb * causal_dwconv(u_hist, conv_w)
    conv_state = u_hist[:, -(CONV_WIDTH - 1):]
    mixed = jax.nn.sigmoid(ga) * (attn @ w_o_attn) + jax.nn.sigmoid(gc) * (conv_out @ w_o_conv)
    h = x + mixed @ w_merge
    hn = rms_norm(h, ffn_norm_w)
    g, val = jnp.split(hn @ w_up, [D_FF], axis=-1)
    g_hist = jnp.concatenate([ffn_prev.astype(x.dtype), g], axis=1)
    g_conv = causal_dwconv(g_hist, ffn_conv_w) + ffn_conv_b
    y = h + (jax.nn.silu(g_conv) * val) @ w_down
    ffn_state = g_hist[:, -(FFN_CONV_WIDTH - 1):]
    return y, k, v, ki, conv_state, ffn_state


def setup_inputs(seed: int = 0) -> dict:
    key = jax.random.key(seed)
    ks = jax.random.split(key, 24)
    n_pages = PAST_LEN // PAGE_SIZE
    n_phys = (DEC_BATCH * n_pages * 5) // 4
    nrm = lambda k, shape, scale=1.0: jax.random.normal(k, shape, jnp.float32) * scale
    page_table = jax.random.permutation(ks[7], n_phys)[:DEC_BATCH * n_pages].reshape(DEC_BATCH, n_pages).astype(jnp.int32)
    return {
        'x_prompt': nrm(ks[0], (BATCH, SEQ, D_MODEL)),
        'x_sample': nrm(ks[1], (DEC_BATCH, DEC_SEQ, D_MODEL)),
        'cache_k': nrm(ks[2], (DEPTH, n_phys, PAGE_SIZE, N_HEADS, HEAD_DIM)),
        'cache_v': nrm(ks[3], (DEPTH, n_phys, PAGE_SIZE, N_HEADS, HEAD_DIM)),
        'cache_kidx': nrm(ks[4], (DEPTH, n_phys, PAGE_SIZE, IDX_DIM)),
        'state_conv': nrm(ks[5], (DEPTH, DEC_BATCH, CONV_WIDTH - 1, CONV_DIM)),
        'state_ffn_conv': nrm(ks[6], (DEPTH, DEC_BATCH, FFN_CONV_WIDTH - 1, D_FF)),
        'page_table': page_table,
        'meta_tokens': nrm(ks[8], (N_META, D_MODEL)),
        'attn_norm_w': 1.0 + nrm(ks[9], (DEPTH, D_MODEL), 0.01),
        'w_in': nrm(ks[10], (DEPTH, D_MODEL, IN_COLS), D_MODEL ** -0.5),
        'q_norm_w': 1.0 + nrm(ks[11], (DEPTH, HEAD_DIM), 0.01),
        'k_norm_w': 1.0 + nrm(ks[12], (DEPTH, HEAD_DIM), 0.01),
        'conv_w': nrm(ks[13], (DEPTH, CONV_WIDTH, CONV_DIM), CONV_WIDTH ** -0.5),
        'w_o_attn': nrm(ks[14], (DEPTH, ATTN_DIM, D_MODEL), ATTN_DIM ** -0.5),
        'w_o_conv': nrm(ks[15], (DEPTH, CONV_DIM, D_MODEL), CONV_DIM ** -0.5),
        'w_merge': nrm(ks[16], (DEPTH, D_MODEL, D_MODEL), D_MODEL ** -0.5),
        'ffn_norm_w': 1.0 + nrm(ks[17], (DEPTH, D_MODEL), 0.01),
        'w_up': nrm(ks[18], (DEPTH, D_MODEL, 2 * D_FF), D_MODEL ** -0.5),
        'ffn_conv_w': nrm(ks[19], (DEPTH, FFN_CONV_WIDTH, D_FF), FFN_CONV_WIDTH ** -0.5),
        'ffn_conv_b': nrm(ks[20], (DEPTH, D_FF), 0.01),
        'w_down': nrm(ks[21], (DEPTH, D_FF, D_MODEL), D_FF ** -0.5),
    }


def reference(x_prompt, x_sample, cache_k, cache_v, cache_kidx, state_conv, state_ffn_conv, page_table,
              meta_tokens, attn_norm_w, w_in, q_norm_w, k_norm_w, conv_w, w_o_attn, w_o_conv, w_merge,
              ffn_norm_w, w_up, ffn_conv_w, ffn_conv_b, w_down):
    B = x_prompt.shape[0]
    DB, S = x_sample.shape[0], x_sample.shape[1]
    past = page_table.shape[1] * PAGE_SIZE
    topk_p = min(TOPK_MAX, x_prompt.shape[1] // 4)
    topk_s = min(TOPK_MAX, (past + S) // 4)
    hp = jnp.concatenate([jnp.broadcast_to(meta_tokens.astype(x_prompt.dtype)[None], (B, N_META, D_MODEL)), x_prompt], axis=1)
    T = hp.shape[1]
    pos_p = jnp.arange(T, dtype=jnp.int32)
    pos_s = past + jnp.arange(S, dtype=jnp.int32)
    hs = x_sample
    kp, vp, kip, cp, fp = [], [], [], [], []
    ksm, vsm, kism, csm, fsm = [], [], [], [], []
    for l in range(DEPTH):
        wl = (attn_norm_w[l], w_in[l], q_norm_w[l], k_norm_w[l], conv_w[l], w_o_attn[l], w_o_conv[l],
              w_merge[l], ffn_norm_w[l], w_up[l], ffn_conv_w[l], ffn_conv_b[l], w_down[l])
        attend_p = functools.partial(prompt_sparse_attention, topk=topk_p)
        hp, k1, v1, ki1, c1, f1 = trunk_layer(
            hp, pos_p, jnp.zeros((B, CONV_WIDTH - 1, CONV_DIM), hp.dtype),
            jnp.zeros((B, FFN_CONV_WIDTH - 1, D_FF), hp.dtype), attend_p, *wl)
        attend_s = functools.partial(sample_sparse_attention, cache_k=cache_k[l], cache_v=cache_v[l],
                                     cache_kidx=cache_kidx[l], page_table=page_table, topk=topk_s)
        hs, k2, v2, ki2, c2, f2 = trunk_layer(hs, pos_s, state_conv[l], state_ffn_conv[l], attend_s, *wl)
        kp.append(k1); vp.append(v1); kip.append(ki1); cp.append(c1); fp.append(f1)
        ksm.append(k2); vsm.append(v2); kism.append(ki2); csm.append(c2); fsm.append(f2)
    y_prompt = hp[:, N_META:]
    y_sample = hs
    return (y_prompt, y_sample,
            jnp.stack(kp), jnp.stack(vp), jnp.stack(kip), jnp.stack(cp), jnp.stack(fp),
            jnp.stack(ksm), jnp.stack(vsm), jnp.stack(kism), jnp.stack(csm), jnp.stack(fsm))
```

```python
import functools

import jax
import jax.numpy as jnp
import numpy as np
from jax import lax
from jax.experimental import pallas as pl
from jax.experimental.pallas import tpu as pltpu

N_META = 16
N_HEADS = 8
HEAD_DIM = 64
ATTN_DIM = N_HEADS * HEAD_DIM
IDX_HEADS = 4
IDX_DIM = 64
TOPK_MAX = 256
CONV_DIM = 512
PAGE_SIZE = 128
ROPE_THETA = 10000.0
EPS = 1e-6
Q_BLOCK = 128
LANES = 128
IDX_K = 256
NEG_BIAS = -1e30
INT_MIN = -(2 ** 31)
VMEM_LIMIT = 56 * 1024 * 1024

_NT = (((1,), (1,)), ((), ()))


def _bf(x):
    return x.astype(jnp.bfloat16)


def _split_hi_lo(x):
    hi = _bf(x)
    lo = _bf(x - hi.astype(jnp.float32))
    return hi, lo


def _dot(a, b):
    return jnp.dot(a, b, preferred_element_type=jnp.float32)


def _dot_nt(a, b):
    return lax.dot_general(a, b, _NT, preferred_element_type=jnp.float32)


def _rms(x, w):
    ms = jnp.mean(x * x, axis=-1, keepdims=True)
    return x * lax.rsqrt(ms + EPS) * w


def _rope_slab(x, cos, sin):
    lane = lax.broadcasted_iota(jnp.int32, x.shape, 1)
    first_half = (lane & (HEAD_DIM - 1)) < (HEAD_DIM // 2)
    swapped = jnp.where(first_half, pltpu.roll(x, LANES - HEAD_DIM // 2, 1), pltpu.roll(x, HEAD_DIM // 2, 1))
    return x * cos + swapped * sin


def _sort_key(s):
    s = jnp.where(s == 0.0, 0.0, s)
    b = pltpu.bitcast(s, jnp.int32)
    return jnp.where(b >= 0, b, b ^ jnp.int32(0x7FFFFFFF))


def _inproj_a_kernel(x_ref, nw_ref, wqkv_ref, whi_ref, wlo_ref, qnw_ref, knw_ref, g_ref, cos_ref, sin_ref,
                     q_ref, kf_ref, kb_ref, vf_ref, vt_ref, qi3_ref, kif_ref, ki3_ref, kiwt_ref, *, emit_vt):
    xn = _rms(x_ref[...], nw_ref[...])
    xhi, xlo = _split_hi_lo(xn)
    cos = cos_ref[...]
    sin = sin_ref[...]
    gmat = g_ref[...]
    n_slab = ATTN_DIM // LANES

    def qk_path(col0, nw):
        y = _dot(xhi, wqkv_ref[:, col0:col0 + ATTN_DIM])
        ss = _dot(_bf(y * y), gmat)
        y = y * lax.rsqrt(ss * (1.0 / HEAD_DIM) + EPS) * nw
        return jnp.concatenate([_rope_slab(y[:, s * LANES:(s + 1) * LANES], cos, sin) for s in range(n_slab)], axis=1)

    q = qk_path(0, qnw_ref[...])
    q_ref[...] = _bf(q * (HEAD_DIM ** -0.5))
    k = qk_path(ATTN_DIM, knw_ref[...])
    kf_ref[...] = k
    kb_ref[...] = _bf(k)
    v = _dot(xhi, wqkv_ref[:, 2 * ATTN_DIM:3 * ATTN_DIM])
    vf_ref[...] = v
    if emit_vt:
        vt = _bf(v.T)
        for c in range(vt.shape[1] // LANES):
            vt_ref[0, c] = vt[:, c * LANES:(c + 1) * LANES]
    else:
        vt_ref[...] = jnp.zeros(vt_ref.shape, vt_ref.dtype)

    idx = _dot(xhi, whi_ref[...]) + _dot(xlo, whi_ref[...]) + _dot(xhi, wlo_ref[...])
    lane = lax.broadcasted_iota(jnp.int32, (idx.shape[0], LANES), 1)
    low = lane < IDX_DIM
    zero = jnp.zeros((idx.shape[0], LANES), jnp.float32)

    def hi_lo_f32(y):
        hi = _bf(y).astype(jnp.float32)
        return hi, y - hi

    pieces = []
    for s in range(IDX_HEADS * IDX_DIM // LANES):
        slab = _rope_slab(idx[:, s * LANES:(s + 1) * LANES], cos, sin)
        hi, lo = hi_lo_f32(slab)
        hi_r, lo_r = hi_lo_f32(pltpu.roll(slab, IDX_DIM, 1))
        pieces += [jnp.where(low, hi, lo_r), jnp.where(low, hi, zero),
                   jnp.where(low, hi_r, lo), jnp.where(low, hi_r, zero)]
    qi3_ref[...] = _bf(jnp.concatenate(pieces, axis=1))
    c3 = idx[:, IDX_HEADS * IDX_DIM:IDX_HEADS * IDX_DIM + LANES]
    ki = _rope_slab(c3, cos, sin)
    kif_ref[...] = ki[:, :IDX_DIM]
    hi, lo = hi_lo_f32(ki)
    hi_r = pltpu.roll(hi, IDX_DIM, 1)
    ki3_ref[...] = _bf(jnp.concatenate([jnp.where(low, hi, hi_r), jnp.where(low, lo, zero)], axis=1))
    kiwt_ref[0] = c3.T


def _inproj_a(x, nw, wqkv, whi, wlo, qnw, knw, gmat, cos, sin, *, n_seq, tm, emit_vt):
    rows = x.shape[0]
    d = x.shape[1]
    tps = rows // n_seq
    tiles = tps // tm
    n_c = max(tm // LANES, 1)
    grid = (rows // tm,)
    row = lambda i: (i, 0)
    const = lambda i: (0, 0)
    tab = lambda i: (i % tiles, 0)
    f32, bf16 = jnp.float32, jnp.bfloat16
    vt_shape = (n_seq, tps // LANES, ATTN_DIM, LANES) if emit_vt else (rows // tm, 1, 16, LANES)
    vt_block = (1, n_c, ATTN_DIM, LANES) if emit_vt else (1, 1, 16, LANES)
    vt_map = (lambda i: (i // tiles, i % tiles, 0, 0)) if emit_vt else (lambda i: (i, 0, 0, 0))
    out_shape = (
        jax.ShapeDtypeStruct((rows, ATTN_DIM), bf16),
        jax.ShapeDtypeStruct((rows, ATTN_DIM), f32),
        jax.ShapeDtypeStruct((rows, ATTN_DIM), bf16),
        jax.ShapeDtypeStruct((rows, ATTN_DIM), f32),
        jax.ShapeDtypeStruct(vt_shape, bf16),
        jax.ShapeDtypeStruct((rows, IDX_HEADS * IDX_K), bf16),
        jax.ShapeDtypeStruct((rows, IDX_DIM), f32),
        jax.ShapeDtypeStruct((rows, IDX_K), bf16),
        jax.ShapeDtypeStruct((n_seq, LANES, tps), f32),
    )
    out_specs = (
        pl.BlockSpec((tm, ATTN_DIM), row), pl.BlockSpec((tm, ATTN_DIM), row), pl.BlockSpec((tm, ATTN_DIM), row),
        pl.BlockSpec((tm, ATTN_DIM), row), pl.BlockSpec(vt_block, vt_map),
        pl.BlockSpec((tm, IDX_HEADS * IDX_K), row), pl.BlockSpec((tm, IDX_DIM), row), pl.BlockSpec((tm, IDX_K), row),
        pl.BlockSpec((1, LANES, tm), lambda i: (i // tiles, 0, i % tiles)),
    )
    in_specs = [
        pl.BlockSpec((tm, d), row), pl.BlockSpec((1, d), const), pl.BlockSpec(wqkv.shape, const),
        pl.BlockSpec(whi.shape, const), pl.BlockSpec(wlo.shape, const),
        pl.BlockSpec((1, ATTN_DIM), const), pl.BlockSpec((1, ATTN_DIM), const), pl.BlockSpec(gmat.shape, const),
        pl.BlockSpec((tm, LANES), tab), pl.BlockSpec((tm, LANES), tab),
    ]
    return pl.pallas_call(
        functools.partial(_inproj_a_kernel, emit_vt=emit_vt),
        out_shape=out_shape, grid=grid, in_specs=in_specs, out_specs=out_specs,
        compiler_params=pltpu.CompilerParams(dimension_semantics=("parallel",), vmem_limit_bytes=VMEM_LIMIT),
        name="inproj_a",
    )(x, nw, wqkv, whi, wlo, qnw, knw, gmat, cos, sin)


def _inproj_b_kernel(x_ref, nw_ref, w_ref, u_ref, cb_ref, sga_ref, sgc_ref):
    xb = _bf(_rms(x_ref[...], nw_ref[...]))
    d = x_ref.shape[1]
    cx = _dot(xb, w_ref[:, 0:CONV_DIM])
    cb_ref[...] = _dot(xb, w_ref[:, CONV_DIM:2 * CONV_DIM])
    cc = _dot(xb, w_ref[:, 2 * CONV_DIM:3 * CONV_DIM])
    u_ref[...] = cc * cx
    sga_ref[...] = jax.nn.sigmoid(_dot(xb, w_ref[:, 3 * CONV_DIM:3 * CONV_DIM + d]))
    sgc_ref[...] = jax.nn.sigmoid(_dot(xb, w_ref[:, 3 * CONV_DIM + d:3 * CONV_DIM + 2 * d]))


def _inproj_b(x, nw, w, *, tm):
    rows, d = x.shape
    row = lambda i: (i, 0)
    const = lambda i: (0, 0)
    f32 = jnp.float32
    return pl.pallas_call(
        _inproj_b_kernel,
        out_shape=(jax.ShapeDtypeStruct((rows, CONV_DIM), f32), jax.ShapeDtypeStruct((rows, CONV_DIM), f32),
                   jax.ShapeDtypeStruct((rows, d), f32), jax.ShapeDtypeStruct((rows, d), f32)),
        grid=(rows // tm,),
        in_specs=[pl.BlockSpec((tm, d), row), pl.BlockSpec((1, d), const), pl.BlockSpec(w.shape, const)],
        out_specs=(pl.BlockSpec((tm, CONV_DIM), row), pl.BlockSpec((tm, CONV_DIM), row),
                   pl.BlockSpec((tm, d), row), pl.BlockSpec((tm, d), row)),
        compiler_params=pltpu.CompilerParams(dimension_semantics=("parallel",), vmem_limit_bytes=VMEM_LIMIT),
        name="inproj_b",
    )(x, nw, w)


def _prompt_attn_kernel(q_ref, k_ref, vt_ref, qi3_ref, ki3_ref, wi_ref, o_ref,
                        s_ref, acc_ref, qz_ref, m_ref, l_ref, *, topk):
    i = pl.program_id(1)
    qb = Q_BLOCK
    f32, i32 = jnp.float32, jnp.int32
    row_io = lax.broadcasted_iota(i32, (qb, qb), 0)
    lane_io = lax.broadcasted_iota(i32, (qb, qb), 1)
    diag_ok = row_io <= lane_io

    qi = qi3_ref[...]
    qstack = jnp.concatenate([qi[:, h * IDX_K:(h + 1) * IDX_K] for h in range(IDX_HEADS)], axis=0)
    w8 = wi_ref[0]

    def score_block(j, diagonal):
        off = pl.multiple_of(j * qb, qb)
        st = _dot_nt(ki3_ref[0, pl.ds(off, qb), :], qstack)
        s = jnp.zeros((qb, qb), f32)
        for h in range(IDX_HEADS):
            s = s + w8[h:h + 1, :] * jnp.maximum(st[:, h * qb:(h + 1) * qb], 0.0)
        if diagonal:
            s = jnp.where(diag_ok, s, -jnp.inf)
        s_ref[pl.ds(off, qb), :] = _sort_key(s)

    def score_body(j, c):
        score_block(j, False)
        return c

    lax.fori_loop(0, i, score_body, 0)
    score_block(i, True)

    def count_ge(cand):
        def body(j, acc):
            off = pl.multiple_of(j * qb, qb)
            blk = s_ref[pl.ds(off, qb), :]
            hit = jnp.where(blk >= cand, 1, 0).astype(i32)
            return acc + jnp.sum(hit.reshape(qb // 8, 8, qb), axis=0)
        acc = lax.fori_loop(0, i + 1, body, jnp.zeros((8, qb), i32))
        return jnp.sum(acc, axis=0, keepdims=True)

    def bisect(step, carry):
        prefix, cnt_rej = carry
        cand = prefix ^ lax.shift_left(jnp.int32(1), 31 - step)
        cnt = count_ge(cand)
        ok = cnt >= topk
        return jnp.where(ok, cand, prefix), jnp.where(ok, cnt_rej, cnt)

    thr, cnt_gt = lax.fori_loop(0, 32, bisect, (jnp.full((1, qb), INT_MIN, i32), jnp.zeros((1, qb), i32)))
    need = (topk - cnt_gt).astype(f32)

    ltri = _bf(jnp.where(lane_io < row_io, 1.0, 0.0))

    def bias_block(j, running, diagonal):
        off = pl.multiple_of(j * qb, qb)
        key = s_ref[pl.ds(off, qb), :]
        eq = key == thr
        e = jnp.where(eq, 1.0, 0.0)
        rank = _dot(ltri, _bf(e)) + running
        sel = (key > thr) | (eq & (rank < need))
        if diagonal:
            sel = sel & diag_ok
        s_ref[pl.ds(off, qb), :] = pltpu.bitcast(jnp.where(sel, 0.0, NEG_BIAS).astype(f32), i32)
        return running + jnp.sum(e, axis=0, keepdims=True)

    running = lax.fori_loop(0, i, lambda j, r: bias_block(j, r, False), jnp.zeros((1, qb), f32))
    bias_block(i, running, True)

    q = q_ref[...].astype(f32)
    low = lane_io < HEAD_DIM
    n_pair = N_HEADS // 2
    for p in range(n_pair):
        qp = q[:, p * LANES:(p + 1) * LANES]
        qz_ref[p] = _bf(jnp.concatenate([jnp.where(low, qp, 0.0), jnp.where(low, 0.0, qp)], axis=0))
    m_ref[...] = jnp.full(m_ref.shape, NEG_BIAS, f32)
    l_ref[...] = jnp.zeros(l_ref.shape, f32)
    acc_ref[...] = jnp.zeros(acc_ref.shape, f32)

    def attn_body(j, c):
        off = pl.multiple_of(j * qb, qb)
        kb = k_ref[0, pl.ds(off, qb), :]
        vb = vt_ref[0, j]
        bias = pltpu.bitcast(s_ref[pl.ds(off, qb), :], f32)
        for p in range(n_pair):
            lg = _dot_nt(kb[:, p * LANES:(p + 1) * LANES], qz_ref[p])
            for hh in range(2):
                h = 2 * p + hh
                x = lg[:, hh * qb:(hh + 1) * qb] + bias
                m_old = m_ref[h:h + 1, :]
                m_new = jnp.maximum(m_old, jnp.max(x, axis=0, keepdims=True))
                alpha = jnp.exp(m_old - m_new)
                pe = jnp.exp(x - m_new)
                l_ref[h:h + 1, :] = alpha * l_ref[h:h + 1, :] + jnp.sum(pe, axis=0, keepdims=True)
                m_ref[h:h + 1, :] = m_new
                pv = _dot(vb[h * HEAD_DIM:(h + 1) * HEAD_DIM, :], _bf(pe))
                acc_ref[h * HEAD_DIM:(h + 1) * HEAD_DIM, :] = acc_ref[h * HEAD_DIM:(h + 1) * HEAD_DIM, :] * alpha + pv
        return c

    lax.fori_loop(0, i + 1, attn_body, 0)
    outs = [acc_ref[h * HEAD_DIM:(h + 1) * HEAD_DIM, :] / l_ref[h:h + 1, :] for h in range(N_HEADS)]
    o_ref[...] = _bf(jnp.concatenate(outs, axis=0).T)


def _prompt_attention(q, kb, vt, qi3, ki3, kiwt, *, n_seq, topk):
    rows = q.shape[0]
    tps = rows // n_seq
    nblk = tps // Q_BLOCK
    kb3 = kb.reshape(n_seq, tps, ATTN_DIM)
    ki33 = ki3.reshape(n_seq, tps, IDX_K)
    qrow = lambda b, i: (b * nblk + i, 0)
    return pl.pallas_call(
        functools.partial(_prompt_attn_kernel, topk=topk),
        out_shape=jax.ShapeDtypeStruct((rows, ATTN_DIM), jnp.bfloat16),
        grid=(n_seq, nblk),
        in_specs=[
            pl.BlockSpec((Q_BLOCK, ATTN_DIM), qrow),
            pl.BlockSpec((1, tps, ATTN_DIM), lambda b, i: (b, 0, 0)),
            pl.BlockSpec((1, nblk, ATTN_DIM, LANES), lambda b, i: (b, 0, 0, 0)),
            pl.BlockSpec((Q_BLOCK, IDX_HEADS * IDX_K), qrow),
            pl.BlockSpec((1, tps, IDX_K), lambda b, i: (b, 0, 0)),
            pl.BlockSpec((1, 8, Q_BLOCK), lambda b, i: (b, IDX_DIM // 8, i)),
        ],
        out_specs=pl.BlockSpec((Q_BLOCK, ATTN_DIM), qrow),
        scratch_shapes=[
            pltpu.VMEM((tps, Q_BLOCK), jnp.int32),
            pltpu.VMEM((ATTN_DIM, Q_BLOCK), jnp.float32),
            pltpu.VMEM((N_HEADS // 2, 2 * Q_BLOCK, LANES), jnp.bfloat16),
            pltpu.VMEM((N_HEADS, Q_BLOCK), jnp.float32),
            pltpu.VMEM((N_HEADS, Q_BLOCK), jnp.float32),
        ],
        compiler_params=pltpu.CompilerParams(dimension_semantics=("parallel", "arbitrary"),
                                             vmem_limit_bytes=VMEM_LIMIT),
        name="prompt_attention",
    )(q, kb3, vt, qi3, ki33, kiwt)


def _decode_rows(n_pages):
    return -(-(n_pages + 1) // LANES) * LANES


def _decode_select_kernel(pt_ref, *refs, pages_per_step, n_pages, topk):
    del pt_ref
    page_refs = refs[:pages_per_step]
    qi_ref, kin_ref, wi_ref, bias_ref, s_ref = refs[pages_per_step:]
    b = pl.program_id(0)
    g = pl.program_id(1)
    n_seq = s_ref.shape[0]
    f32, i32 = jnp.float32, jnp.int32
    rows = s_ref.shape[1]

    qf = qi_ref[0]
    wcol = wi_ref[0]
    q_hi, q_lo = _split_hi_lo(qf)

    def scores_of(keys):
        k_hi, k_lo = _split_hi_lo(keys)
        st = _dot_nt(q_hi, k_hi) + _dot_nt(q_lo, k_hi) + _dot_nt(q_hi, k_lo)
        return jnp.sum(wcol * jnp.maximum(st, 0.0), axis=0, keepdims=True)

    for p in range(pages_per_step):
        s_ref[b, pl.ds(g * pages_per_step + p, 1), :] = _sort_key(scores_of(page_refs[p][0]))

    @pl.when(g == 0)
    def _():
        s_new = scores_of(kin_ref[0])
        lane = lax.broadcasted_iota(i32, (rows - n_pages, LANES), 1)
        sub = lax.broadcasted_iota(i32, (rows - n_pages, LANES), 0)
        tail = jnp.where((lane == 0) & (sub == 0), jnp.broadcast_to(s_new, (rows - n_pages, LANES)), -jnp.inf)
        s_ref[b, n_pages:rows, :] = _sort_key(tail)

    @pl.when((b == n_seq - 1) & (g == pl.num_programs(1) - 1))
    def _():
        keys = s_ref[...]

        def count_ge(cand):
            hit = jnp.where(keys >= cand, 1, 0).astype(i32)
            return jnp.sum(jnp.sum(hit, axis=1, keepdims=True), axis=2, keepdims=True)

        def bisect(step, carry):
            prefix, cnt_rej = carry
            cand = prefix ^ lax.shift_left(jnp.int32(1), 31 - step)
            cnt = count_ge(cand)
            ok = cnt >= topk
            return jnp.where(ok, cand, prefix), jnp.where(ok, cnt_rej, cnt)

        thr, cnt_gt = lax.fori_loop(0, 32, bisect, (jnp.full((n_seq, 1, 1), INT_MIN, i32),
                                                     jnp.zeros((n_seq, 1, 1), i32)))
        need = (topk - cnt_gt).astype(f32)
        eq = keys == thr
        e = jnp.where(eq, 1.0, 0.0)
        e2 = _bf(e.reshape(n_seq * rows, LANES))
        a_io = lax.broadcasted_iota(i32, (LANES, LANES), 0)
        b_io = lax.broadcasted_iota(i32, (LANES, LANES), 1)
        upper = _bf(jnp.where(a_io < b_io, 1.0, 0.0))
        in_row = _dot(e2, upper).reshape(n_seq, rows, LANES)
        row_tot = _dot(e2, jnp.ones((LANES, LANES), jnp.bfloat16)).reshape(n_seq, rows, LANES)
        r_io = lax.broadcasted_iota(i32, (rows, rows), 0)
        c_io = lax.broadcasted_iota(i32, (rows, rows), 1)
        lower = _bf(jnp.where(c_io < r_io, 1.0, 0.0))
        for s in range(n_seq):
            before = _dot(lower, _bf(row_tot[s]))
            rank = in_row[s] + before
            sel = (keys[s] > thr[s]) | (eq[s] & (rank < need[s]))
            bias_ref[s] = jnp.where(sel, 0.0, NEG_BIAS).astype(f32)


def _decode_select(page_table, cache_kidx, qi_rows, ki_new, wi_rows, *, topk, pages_per_step):
    n_seq, n_pages = page_table.shape
    steps = n_pages // pages_per_step
    rows = _decode_rows(n_pages)

    def page_map(p):
        return lambda b, g, pt: (pt[b, g * pages_per_step + p], 0, 0)

    in_specs = [pl.BlockSpec((1, PAGE_SIZE, IDX_DIM), page_map(p)) for p in range(pages_per_step)]
    in_specs += [
        pl.BlockSpec((1, 16, IDX_DIM), lambda b, g, pt: (b, 0, 0)),
        pl.BlockSpec((1, PAGE_SIZE, IDX_DIM), lambda b, g, pt: (b, 0, 0)),
        pl.BlockSpec((1, 16, 1), lambda b, g, pt: (b, 0, 0)),
    ]
    grid_spec = pltpu.PrefetchScalarGridSpec(
        num_scalar_prefetch=1, grid=(n_seq, steps), in_specs=in_specs,
        out_specs=pl.BlockSpec((n_seq, rows, LANES), lambda b, g, pt: (0, 0, 0)),
        scratch_shapes=[pltpu.VMEM((n_seq, rows, LANES), jnp.int32)],
    )
    return pl.pallas_call(
        functools.partial(_decode_select_kernel, pages_per_step=pages_per_step, n_pages=n_pages, topk=topk),
        out_shape=jax.ShapeDtypeStruct((n_seq, rows, LANES), jnp.float32),
        grid_spec=grid_spec,
        compiler_params=pltpu.CompilerParams(dimension_semantics=("arbitrary", "arbitrary"),
                                             vmem_limit_bytes=VMEM_LIMIT),
        name="decode_select",
    )(page_table, *([cache_kidx] * pages_per_step), qi_rows, ki_new, wi_rows)


def _decode_attn_kernel(pt_ref, *refs, pages_per_step, n_pages):
    del pt_ref
    k_refs = refs[:pages_per_step]
    v_refs = refs[pages_per_step:2 * pages_per_step]
    q_ref, kn_ref, vn_ref, bias_ref, o_ref, m_ref, l_ref, acc_ref = refs[2 * pages_per_step:]
    g = pl.program_id(1)
    f32 = jnp.float32
    hp = m_ref.shape[0]
    r_io = lax.broadcasted_iota(jnp.int32, (hp, ATTN_DIM), 0)
    c_io = lax.broadcasted_iota(jnp.int32, (hp, ATTN_DIM), 1)
    head_mask = (c_io // HEAD_DIM) == r_io
    qf = jnp.where(head_mask, jnp.broadcast_to(q_ref[0].astype(f32), (hp, ATTN_DIM)), 0.0)
    qbd = _bf(qf)

    @pl.when(g == 0)
    def _():
        m_ref[...] = jnp.full(m_ref.shape, NEG_BIAS, f32)
        l_ref[...] = jnp.zeros(l_ref.shape, f32)
        acc_ref[...] = jnp.zeros(acc_ref.shape, f32)

    def update(x, pv_of):
        m_old = m_ref[...]
        m_new = jnp.maximum(m_old, jnp.max(x, axis=1, keepdims=True))
        alpha = jnp.exp(m_old - m_new)
        pe = jnp.exp(x - m_new)
        l_ref[...] = alpha * l_ref[...] + jnp.sum(pe, axis=1, keepdims=True)
        m_ref[...] = m_new
        acc_ref[...] = acc_ref[...] * alpha + pv_of(pe)

    for p in range(pages_per_step):
        keys, vals = k_refs[p][0], v_refs[p][0]
        bias = bias_ref[0, pl.ds(g * pages_per_step + p, 1), :]
        update(_dot_nt(qbd, _bf(keys)) + bias, lambda pe, vals=vals: _dot(_bf(pe), _bf(vals)))

    @pl.when(g == pl.num_programs(1) - 1)
    def _():
        x_new = jnp.sum(qf * kn_ref[0], axis=1, keepdims=True) + bias_ref[0, n_pages:n_pages + 1, 0:1]
        update(x_new, lambda pe: pe * vn_ref[0])
        full = acc_ref[...] / l_ref[...]
        o_ref[0] = jnp.sum(jnp.where(head_mask, full, 0.0), axis=0, keepdims=True)


def _decode_attention(page_table, cache_k, cache_v, q_rows, k_new, v_new, bias, *, pages_per_step):
    n_seq, n_pages = page_table.shape
    steps = n_pages // pages_per_step
    rows = _decode_rows(n_pages)
    hp = 2 * N_HEADS

    def page_map(p):
        return lambda b, g, pt: (pt[b, g * pages_per_step + p], 0, 0)

    page_specs = [pl.BlockSpec((1, PAGE_SIZE, ATTN_DIM), page_map(p)) for p in range(pages_per_step)]
    per_seq = lambda b, g, pt: (b, 0, 0)
    in_specs = page_specs + page_specs + [
        pl.BlockSpec((1, 1, ATTN_DIM), per_seq),
        pl.BlockSpec((1, 1, ATTN_DIM), per_seq),
        pl.BlockSpec((1, 1, ATTN_DIM), per_seq),
        pl.BlockSpec((1, rows, LANES), per_seq),
    ]
    grid_spec = pltpu.PrefetchScalarGridSpec(
        num_scalar_prefetch=1, grid=(n_seq, steps), in_specs=in_specs,
        out_specs=pl.BlockSpec((1, 1, ATTN_DIM), per_seq),
        scratch_shapes=[pltpu.VMEM((hp, 1), jnp.float32), pltpu.VMEM((hp, 1), jnp.float32),
                        pltpu.VMEM((hp, ATTN_DIM), jnp.float32)],
    )
    return pl.pallas_call(
        functools.partial(_decode_attn_kernel, pages_per_step=pages_per_step, n_pages=n_pages),
        out_shape=jax.ShapeDtypeStruct((n_seq, 1, ATTN_DIM), jnp.float32),
        grid_spec=grid_spec,
        compiler_params=pltpu.CompilerParams(dimension_semantics=("parallel", "arbitrary"),
                                             vmem_limit_bytes=VMEM_LIMIT),
        name="decode_attention",
    )(page_table, *([cache_k] * pages_per_step), *([cache_v] * pages_per_step), q_rows, k_new, v_new, bias)


def _shifted(u, halo, shift):
    rolled = pltpu.roll(u, shift, 0)
    row = lax.broadcasted_iota(jnp.int32, u.shape, 0)
    out = rolled
    for r in range(shift):
        out = jnp.where(row == r, halo[8 - shift + r:8 - shift + r + 1, :], out)
    return out


def _merge_kernel(*refs, sequential, tiles):
    if sequential:
        x_ref, a_ref, u_ref, cb_ref, sga_ref, sgc_ref, cw_ref, woa_ref, woc_ref, wm_ref, h_ref, halo_ref = refs
        u = u_ref[...]

        @pl.when(pl.program_id(0) % tiles == 0)
        def _():
            halo_ref[...] = jnp.zeros(halo_ref.shape, halo_ref.dtype)

        halo = halo_ref[...]
        u1 = _shifted(u, halo, 1)
        u2 = _shifted(u, halo, 2)
        halo_ref[...] = u[u.shape[0] - 8:, :]
    else:
        x_ref, a_ref, u_ref, cb_ref, sga_ref, sgc_ref, cw_ref, woa_ref, woc_ref, wm_ref, s0_ref, s1_ref, h_ref = refs
        u = u_ref[...]
        u2 = s0_ref[...]
        u1 = s1_ref[...]
    cw = cw_ref[...]
    conv = cb_ref[...] * (u2 * cw[0:1, :] + u1 * cw[1:2, :] + u * cw[2:3, :])
    mixed = sga_ref[...] * _dot(a_ref[...], woa_ref[...]) + sgc_ref[...] * _dot(_bf(conv), woc_ref[...])
    h_ref[...] = x_ref[...] + _dot(_bf(mixed), wm_ref[...])


def _merge(x, attn, u, cb, sga, sgc, conv_w, woa, woc, wm, states, *, tm, tiles):
    rows, d = x.shape
    sequential = states is None
    row = lambda i: (i, 0)
    const = lambda i: (0, 0)
    in_specs = [pl.BlockSpec((tm, d), row), pl.BlockSpec((tm, ATTN_DIM), row), pl.BlockSpec((tm, CONV_DIM), row),
                pl.BlockSpec((tm, CONV_DIM), row), pl.BlockSpec((tm, d), row), pl.BlockSpec((tm, d), row),
                pl.BlockSpec(conv_w.shape, const), pl.BlockSpec(woa.shape, const), pl.BlockSpec(woc.shape, const),
                pl.BlockSpec(wm.shape, const)]
    args = [x, attn, u, cb, sga, sgc, conv_w, woa, woc, wm]
    scratch = []
    if sequential:
        scratch = [pltpu.VMEM((8, CONV_DIM), jnp.float32)]
    else:
        in_specs += [pl.BlockSpec((tm, CONV_DIM), row), pl.BlockSpec((tm, CONV_DIM), row)]
        args += list(states)
    return pl.pallas_call(
        functools.partial(_merge_kernel, sequential=sequential, tiles=tiles),
        out_shape=jax.ShapeDtypeStruct((rows, d), jnp.float32),
        grid=(rows // tm,), in_specs=in_specs, out_specs=pl.BlockSpec((tm, d), row), scratch_shapes=scratch,
        compiler_params=pltpu.CompilerParams(dimension_semantics=("arbitrary",), vmem_limit_bytes=VMEM_LIMIT),
        name="merge",
    )(*args)


def _ffn_kernel(*refs, sequential, tiles, tail_off, tail_rows):
    if sequential:
        h_ref, nw_ref, wg_ref, wv_ref, wd_ref, cw_ref, cbias_ref, y_ref, gt_ref, hn_ref, halo_ref = refs
    else:
        h_ref, nw_ref, wg_ref, wv_ref, wd_ref, cw_ref, cbias_ref, s0_ref, s1_ref, y_ref, gt_ref, hn_ref = refs
    c = pl.program_id(1)

    @pl.when(c == 0)
    def _():
        hn_ref[...] = _bf(_rms(h_ref[...], nw_ref[...]))

    hn = hn_ref[...]
    g = _dot(hn, wg_ref[...])
    val = _dot(hn, wv_ref[...])
    gt_ref[0] = g[tail_off:tail_off + tail_rows, :]
    if sequential:
        @pl.when(pl.program_id(0) % tiles == 0)
        def _():
            halo_ref[c] = jnp.zeros(halo_ref.shape[1:], halo_ref.dtype)

        halo = halo_ref[c]
        g1 = _shifted(g, halo, 1)
        g2 = _shifted(g, halo, 2)
        halo_ref[c] = g[g.shape[0] - 8:, :]
    else:
        g2 = s0_ref[...]
        g1 = s1_ref[...]
    cw = cw_ref[...]
    gc = g2 * cw[0:1, :] + g1 * cw[1:2, :] + g * cw[2:3, :] + cbias_ref[...]
    act = gc * jax.nn.sigmoid(gc) * val
    part = _dot(_bf(act), wd_ref[...])

    @pl.when(c == 0)
    def _():
        y_ref[...] = h_ref[...] + part

    @pl.when(c != 0)
    def _():
        y_ref[...] = y_ref[...] + part


def _ffn(h, nw, wg, wv, wd, conv_w, conv_b, states, *, tm, tiles, n_chunk, tail_off, tail_rows):
    rows, d = h.shape
    dff = wg.shape[1]
    ch = dff // n_chunk
    sequential = states is None
    row = lambda i, c: (i, 0)
    in_specs = [pl.BlockSpec((tm, d), row), pl.BlockSpec((1, d), lambda i, c: (0, 0)),
                pl.BlockSpec((d, ch), lambda i, c: (0, c)), pl.BlockSpec((d, ch), lambda i, c: (0, c)),
                pl.BlockSpec((ch, d), lambda i, c: (c, 0)), pl.BlockSpec((conv_w.shape[0], ch), lambda i, c: (0, c)),
                pl.BlockSpec((1, ch), lambda i, c: (0, c))]
    args = [h, nw, wg, wv, wd, conv_w, conv_b]
    scratch = [pltpu.VMEM((tm, d), jnp.bfloat16)]
    if sequential:
        scratch.append(pltpu.VMEM((n_chunk, 8, ch), jnp.float32))
    else:
        in_specs += [pl.BlockSpec((tm, ch), lambda i, c: (i, c)), pl.BlockSpec((tm, ch), lambda i, c: (i, c))]
        args += list(states)
    n_tiles = rows // tm
    return pl.pallas_call(
        functools.partial(_ffn_kernel, sequential=sequential, tiles=tiles, tail_off=tail_off, tail_rows=tail_rows),
        out_shape=(jax.ShapeDtypeStruct((rows, d), jnp.float32),
                   jax.ShapeDtypeStruct((n_tiles, tail_rows, dff), jnp.float32)),
        grid=(n_tiles, n_chunk), in_specs=in_specs,
        out_specs=(pl.BlockSpec((tm, d), row), pl.BlockSpec((1, tail_rows, ch), lambda i, c: (i, 0, c))),
        scratch_shapes=scratch,
        compiler_params=pltpu.CompilerParams(dimension_semantics=("arbitrary", "arbitrary"),
                                             vmem_limit_bytes=VMEM_LIMIT),
        name="ffn",
    )(*args)


def _rope_tables(pos):
    half = HEAD_DIM // 2
    freqs = ROPE_THETA ** (-jnp.arange(half, dtype=jnp.float32) / half)
    ang = pos.astype(jnp.float32)[:, None] * freqs[None, :]
    cos, sin = jnp.cos(ang), jnp.sin(ang)
    reps = LANES // HEAD_DIM
    return jnp.concatenate([cos, cos] * reps, axis=1), jnp.concatenate([-sin, sin] * reps, axis=1)


def _pick_tile(nblk, cap):
    return max(dd for dd in range(1, cap + 1) if nblk % dd == 0) * Q_BLOCK


def kernel(x_prompt, x_sample, cache_k, cache_v, cache_kidx, state_conv, state_ffn_conv, page_table, meta_tokens, attn_norm_w, w_in, q_norm_w, k_norm_w, conv_w, w_o_attn, w_o_conv, w_merge, ffn_norm_w, w_up, ffn_conv_w, ffn_conv_b, w_down):
    f32 = jnp.float32
    n_b, seq, d = x_prompt.shape
    n_db, n_ds = x_sample.shape[0], x_sample.shape[1]
    depth = w_in.shape[0]
    assert depth == 1 and n_ds == 1
    n_pages = page_table.shape[1]
    past = n_pages * PAGE_SIZE
    t_len = seq + N_META
    assert t_len % 8 == 0
    topk_p = min(TOPK_MAX, seq // 4)
    topk_s = min(TOPK_MAX, (past + n_ds) // 4)
    nblk = -(-t_len // Q_BLOCK)
    tp = nblk * Q_BLOCK
    tm = _pick_tile(nblk, 5)
    tiles = tp // tm
    dff = w_down.shape[1]
    n_chunk = 2 if dff % 256 == 0 else 1

    l = 0
    o_idx = 3 * ATTN_DIM
    n_idx = IDX_HEADS * IDX_DIM + IDX_DIM + IDX_HEADS
    o_conv = o_idx + n_idx
    wqkv = _bf(w_in[l][:, :o_idx])
    w_idx = jnp.pad(w_in[l][:, o_idx:o_conv], ((0, 0), (0, 3 * LANES - n_idx)))
    whi, wlo = _split_hi_lo(w_idx)
    w_b = _bf(w_in[l][:, o_conv:])
    woa, woc, wmg = _bf(w_o_attn[l]), _bf(w_o_conv[l]), _bf(w_merge[l])
    wg, wv, wd = _bf(w_up[l][:, :dff]), _bf(w_up[l][:, dff:]), _bf(w_down[l])
    nw = attn_norm_w[l][None, :]
    fnw = ffn_norm_w[l][None, :]
    qnw = jnp.tile(q_norm_w[l], N_HEADS)[None, :]
    knw = jnp.tile(k_norm_w[l], N_HEADS)[None, :]
    hid = np.arange(ATTN_DIM) // HEAD_DIM
    gmat = jnp.asarray(hid[:, None] == hid[None, :], jnp.bfloat16)
    fcb = ffn_conv_b[l][None, :]

    hp = jnp.concatenate([jnp.broadcast_to(meta_tokens.astype(f32)[None], (n_b, N_META, d)), x_prompt,
                          jnp.zeros((n_b, tp - t_len, d), f32)], axis=1).reshape(n_b * tp, d)
    cos_p, sin_p = _rope_tables(jnp.arange(tp, dtype=jnp.int32))
    q, kf, kb, vf, vt, qi3, kif, ki3, kiwt = _inproj_a(hp, nw, wqkv, whi, wlo, qnw, knw, gmat, cos_p, sin_p,
                                                        n_seq=n_b, tm=tm, emit_vt=True)
    u, cb, sga, sgc = _inproj_b(hp, nw, w_b, tm=tm)
    attn = _prompt_attention(q, kb, vt, qi3, ki3, kiwt, n_seq=n_b, topk=topk_p)
    h = _merge(hp, attn, u, cb, sga, sgc, conv_w[l], woa, woc, wmg, None, tm=tm, tiles=tiles)
    tail_tile, tail_off = (t_len - 8) // tm, (t_len - 8) % tm
    y, g_tail = _ffn(h, fnw, wg, wv, wd, ffn_conv_w[l], fcb, None, tm=tm, tiles=tiles, n_chunk=n_chunk,
                     tail_off=tail_off, tail_rows=8)

    y_prompt = y.reshape(n_b, tp, d)[:, N_META:t_len]
    new_k_p = kf.reshape(n_b, tp, N_HEADS, HEAD_DIM)[:, :t_len][None]
    new_v_p = vf.reshape(n_b, tp, N_HEADS, HEAD_DIM)[:, :t_len][None]
    new_ki_p = kif.reshape(n_b, tp, IDX_DIM)[:, :t_len][None]
    new_conv_p = u.reshape(n_b, tp, CONV_DIM)[:, t_len - 2:t_len][None]
    new_ffn_p = g_tail.reshape(n_b, tiles, 8, dff)[:, tail_tile, 6:8][None]

    xs = x_sample.reshape(n_db, d)
    n_pad = -(-n_db // LANES) * LANES
    xs_pad = jnp.pad(xs, ((0, n_pad - n_db), (0, 0)))
    cos_s, sin_s = _rope_tables(jnp.full((n_pad,), past, jnp.int32))
    qs, kfs, _, vfs, _, qi3s, kifs, _, kiwts = _inproj_a(xs_pad, nw, wqkv, whi, wlo, qnw, knw, gmat, cos_s, sin_s,
                                                          n_seq=1, tm=n_pad, emit_vt=False)
    qs, kfs, vfs, kifs = qs[:n_db], kfs[:n_db], vfs[:n_db], kifs[:n_db]
    us, cbs, sgas, sgcs = _inproj_b(xs, nw, w_b, tm=n_db)
    qi4 = qi3s[:n_db].astype(f32).reshape(n_db, IDX_HEADS, IDX_K // IDX_DIM, IDX_DIM)
    qi_rows = jnp.pad(qi4[:, :, 0] + qi4[:, :, 1], ((0, 0), (0, 16 - IDX_HEADS), (0, 0)))
    wi_rows = jnp.pad(kiwts[0, IDX_DIM:IDX_DIM + IDX_HEADS, :n_db].T, ((0, 0), (0, 16 - IDX_HEADS)))[:, :, None]
    ki_new = jnp.broadcast_to(kifs[:, None, :], (n_db, PAGE_SIZE, IDX_DIM))
    pps_sel = 16 if n_pages % 16 == 0 else 1
    pps_att = 8 if n_pages % 8 == 0 else 1
    bias = _decode_select(page_table, cache_kidx[l], qi_rows, ki_new, wi_rows, topk=topk_s, pages_per_step=pps_sel)
    cache_k2 = cache_k[l].reshape(-1, PAGE_SIZE, ATTN_DIM)
    cache_v2 = cache_v[l].reshape(-1, PAGE_SIZE, ATTN_DIM)
    attn_s = _decode_attention(page_table, cache_k2, cache_v2, qs[:, None, :], kfs[:, None, :], vfs[:, None, :], bias,
                               pages_per_step=pps_att)
    sc = state_conv[l]
    hs = _merge(xs, _bf(attn_s.reshape(n_db, ATTN_DIM)), us, cbs, sgas, sgcs, conv_w[l], woa, woc, wmg,
                (sc[:, 0], sc[:, 1]), tm=n_db, tiles=1)
    sf = state_ffn_conv[l]
    ys, g_s = _ffn(hs, fnw, wg, wv, wd, ffn_conv_w[l], fcb, (sf[:, 0], sf[:, 1]), tm=n_db, tiles=1, n_chunk=n_chunk,
                   tail_off=0, tail_rows=n_db)

    y_sample = ys.reshape(n_db, n_ds, d)
    new_k_s = kfs.reshape(1, n_db, n_ds, N_HEADS, HEAD_DIM)
    new_v_s = vfs.reshape(1, n_db, n_ds, N_HEADS, HEAD_DIM)
    new_ki_s = kifs.reshape(1, n_db, n_ds, IDX_DIM)
    new_conv_s = jnp.stack([sc[:, 1], us], axis=1)[None]
    new_ffn_s = jnp.stack([sf[:, 1], g_s[0]], axis=1)[None]
    return (y_prompt, y_sample, new_k_p, new_v_p, new_ki_p, new_conv_p, new_ffn_p,
            new_k_s, new_v_s, new_ki_s, new_conv_s, new_ffn_s)
```

```python
import functools

import jax
import jax.numpy as jnp
import numpy as np
from jax import lax
from jax.experimental import pallas as pl
from jax.experimental.pallas import tpu as pltpu

N_META = 16
N_HEADS = 8
HEAD_DIM = 64
ATTN_DIM = N_HEADS * HEAD_DIM
IDX_HEADS = 4
IDX_DIM = 64
TOPK_MAX = 256
CONV_DIM = 512
PAGE_SIZE = 128
ROPE_THETA = 10000.0
EPS = 1e-6
Q_BLOCK = 128
KEY_CHUNK = 512
LANES = 128
LOG2_E = 1.4426950408889634
IDX_K = 256
NEG_BIAS = -1e30
INT_MIN = -(2 ** 31)
VMEM_LIMIT = 56 * 1024 * 1024

_NT = (((1,), (1,)), ((), ()))


def _bf(x):
    return x.astype(jnp.bfloat16)


def _split_hi_lo(x):
    hi = _bf(x)
    lo = _bf(x - hi.astype(jnp.float32))
    return hi, lo


def _dot(a, b):
    return jnp.dot(a, b, preferred_element_type=jnp.float32)


def _dot_nt(a, b):
    return lax.dot_general(a, b, _NT, preferred_element_type=jnp.float32)


def _rms(x, w):
    ms = jnp.mean(x * x, axis=-1, keepdims=True)
    return x * lax.rsqrt(ms + EPS) * w


def _rope_slab(x, cos, sin):
    lane = lax.broadcasted_iota(jnp.int32, x.shape, 1)
    first_half = (lane & (HEAD_DIM - 1)) < (HEAD_DIM // 2)
    swapped = jnp.where(first_half, pltpu.roll(x, LANES - HEAD_DIM // 2, 1), pltpu.roll(x, HEAD_DIM // 2, 1))
    return x * cos + swapped * sin


def _sort_key(s):
    s = jnp.where(s == 0.0, 0.0, s)
    b = pltpu.bitcast(s, jnp.int32)
    return jnp.where(b >= 0, b, b ^ jnp.int32(0x7FFFFFFF))


def _inproj_a_kernel(x_ref, nw_ref, wqkv_ref, whi_ref, wlo_ref, qnw_ref, knw_ref, g_ref, cos_ref, sin_ref,
                     q_ref, kf_ref, kb_ref, vf_ref, vt_ref, qi3_ref, kif_ref, ki3_ref, kiwt_ref, *, emit_vt):
    xn = _rms(x_ref[...], nw_ref[...])
    xhi, xlo = _split_hi_lo(xn)
    cos = cos_ref[...]
    sin = sin_ref[...]
    gmat = g_ref[...]
    n_slab = ATTN_DIM // LANES

    def qk_path(col0, nw):
        y = _dot(xhi, wqkv_ref[:, col0:col0 + ATTN_DIM])
        ss = _dot(_bf(y * y), gmat)
        y = y * lax.rsqrt(ss * (1.0 / HEAD_DIM) + EPS) * nw
        return jnp.concatenate([_rope_slab(y[:, s * LANES:(s + 1) * LANES], cos, sin) for s in range(n_slab)], axis=1)

    q = qk_path(0, qnw_ref[...])
    q_ref[...] = _bf(q * (HEAD_DIM ** -0.5 * LOG2_E))
    k = qk_path(ATTN_DIM, knw_ref[...])
    kf_ref[...] = k
    kb_ref[...] = _bf(k)
    v = _dot(xhi, wqkv_ref[:, 2 * ATTN_DIM:3 * ATTN_DIM])
    vf_ref[...] = v
    if emit_vt:
        vt_ref[0, 0] = _bf(v.T)
    else:
        vt_ref[...] = jnp.zeros(vt_ref.shape, vt_ref.dtype)

    idx = _dot(xhi, whi_ref[...]) + _dot(xlo, whi_ref[...]) + _dot(xhi, wlo_ref[...])
    lane = lax.broadcasted_iota(jnp.int32, (idx.shape[0], LANES), 1)
    low = lane < IDX_DIM
    zero = jnp.zeros((idx.shape[0], LANES), jnp.float32)

    def hi_lo_f32(y):
        hi = _bf(y).astype(jnp.float32)
        return hi, y - hi

    pieces = []
    for s in range(IDX_HEADS * IDX_DIM // LANES):
        slab = _rope_slab(idx[:, s * LANES:(s + 1) * LANES], cos, sin)
        hi, lo = hi_lo_f32(slab)
        hi_r, lo_r = hi_lo_f32(pltpu.roll(slab, IDX_DIM, 1))
        pieces += [jnp.where(low, hi, lo_r), jnp.where(low, hi, zero),
                   jnp.where(low, hi_r, lo), jnp.where(low, hi_r, zero)]
    qi3_ref[...] = _bf(jnp.concatenate(pieces, axis=1))
    c3 = idx[:, IDX_HEADS * IDX_DIM:IDX_HEADS * IDX_DIM + LANES]
    ki = _rope_slab(c3, cos, sin)
    kif_ref[...] = ki[:, :IDX_DIM]
    hi, lo = hi_lo_f32(ki)
    hi_r = pltpu.roll(hi, IDX_DIM, 1)
    ki3_ref[...] = _bf(jnp.concatenate([jnp.where(low, hi, hi_r), jnp.where(low, lo, zero)], axis=1))
    kiwt_ref[0] = c3.T


def _inproj_a(x, nw, wqkv, whi, wlo, qnw, knw, gmat, cos, sin, *, n_seq, tm, emit_vt):
    rows = x.shape[0]
    d = x.shape[1]
    tps = rows // n_seq
    tiles = tps // tm
    grid = (rows // tm,)
    row = lambda i: (i, 0)
    const = lambda i: (0, 0)
    tab = lambda i: (i % tiles, 0)
    f32, bf16 = jnp.float32, jnp.bfloat16
    vt_shape = (n_seq, tiles, ATTN_DIM, tm) if emit_vt else (rows // tm, 1, 16, LANES)
    vt_block = (1, 1, ATTN_DIM, tm) if emit_vt else (1, 1, 16, LANES)
    vt_map = (lambda i: (i // tiles, i % tiles, 0, 0)) if emit_vt else (lambda i: (i, 0, 0, 0))
    out_shape = (
        jax.ShapeDtypeStruct((rows, ATTN_DIM), bf16),
        jax.ShapeDtypeStruct((rows, ATTN_DIM), f32),
        jax.ShapeDtypeStruct((rows, ATTN_DIM), bf16),
        jax.ShapeDtypeStruct((rows, ATTN_DIM), f32),
        jax.ShapeDtypeStruct(vt_shape, bf16),
        jax.ShapeDtypeStruct((rows, IDX_HEADS * IDX_K), bf16),
        jax.ShapeDtypeStruct((rows, IDX_DIM), f32),
        jax.ShapeDtypeStruct((rows, IDX_K), bf16),
        jax.ShapeDtypeStruct((n_seq, LANES, tps), f32),
    )
    out_specs = (
        pl.BlockSpec((tm, ATTN_DIM), row), pl.BlockSpec((tm, ATTN_DIM), row), pl.BlockSpec((tm, ATTN_DIM), row),
        pl.BlockSpec((tm, ATTN_DIM), row), pl.BlockSpec(vt_block, vt_map),
        pl.BlockSpec((tm, IDX_HEADS * IDX_K), row), pl.BlockSpec((tm, IDX_DIM), row), pl.BlockSpec((tm, IDX_K), row),
        pl.BlockSpec((1, LANES, tm), lambda i: (i // tiles, 0, i % tiles)),
    )
    in_specs = [
        pl.BlockSpec((tm, d), row), pl.BlockSpec((1, d), const), pl.BlockSpec(wqkv.shape, const),
        pl.BlockSpec(whi.shape, const), pl.BlockSpec(wlo.shape, const),
        pl.BlockSpec((1, ATTN_DIM), const), pl.BlockSpec((1, ATTN_DIM), const), pl.BlockSpec(gmat.shape, const),
        pl.BlockSpec((tm, LANES), tab), pl.BlockSpec((tm, LANES), tab),
    ]
    return pl.pallas_call(
        functools.partial(_inproj_a_kernel, emit_vt=emit_vt),
        out_shape=out_shape, grid=grid, in_specs=in_specs, out_specs=out_specs,
        compiler_params=pltpu.CompilerParams(dimension_semantics=("parallel",), vmem_limit_bytes=VMEM_LIMIT),
        name="inproj_a",
    )(x, nw, wqkv, whi, wlo, qnw, knw, gmat, cos, sin)


def _inproj_b_kernel(x_ref, nw_ref, w_ref, u_ref, cb_ref, sga_ref, sgc_ref):
    xb = _bf(_rms(x_ref[...], nw_ref[...]))
    d = x_ref.shape[1]
    cx = _dot(xb, w_ref[:, 0:CONV_DIM])
    cb_ref[...] = _dot(xb, w_ref[:, CONV_DIM:2 * CONV_DIM])
    cc = _dot(xb, w_ref[:, 2 * CONV_DIM:3 * CONV_DIM])
    u_ref[...] = cc * cx
    sga_ref[...] = jax.nn.sigmoid(_dot(xb, w_ref[:, 3 * CONV_DIM:3 * CONV_DIM + d]))
    sgc_ref[...] = jax.nn.sigmoid(_dot(xb, w_ref[:, 3 * CONV_DIM + d:3 * CONV_DIM + 2 * d]))


def _inproj_b(x, nw, w, *, tm):
    rows, d = x.shape
    row = lambda i: (i, 0)
    const = lambda i: (0, 0)
    f32 = jnp.float32
    return pl.pallas_call(
        _inproj_b_kernel,
        out_shape=(jax.ShapeDtypeStruct((rows, CONV_DIM), f32), jax.ShapeDtypeStruct((rows, CONV_DIM), f32),
                   jax.ShapeDtypeStruct((rows, d), f32), jax.ShapeDtypeStruct((rows, d), f32)),
        grid=(rows // tm,),
        in_specs=[pl.BlockSpec((tm, d), row), pl.BlockSpec((1, d), const), pl.BlockSpec(w.shape, const)],
        out_specs=(pl.BlockSpec((tm, CONV_DIM), row), pl.BlockSpec((tm, CONV_DIM), row),
                   pl.BlockSpec((tm, d), row), pl.BlockSpec((tm, d), row)),
        compiler_params=pltpu.CompilerParams(dimension_semantics=("parallel",), vmem_limit_bytes=VMEM_LIMIT),
        name="inproj_b",
    )(x, nw, w)


def _prompt_attn_kernel(q_ref, k_ref, vt_ref, qi3_ref, ki3_ref, wi_ref, o_ref,
                        s_ref, acc_ref, qz_ref, m_ref, l_ref, *, topk, n_real):
    i = pl.program_id(1)

    @pl.when(i >= n_real)
    def _():
        o_ref[...] = jnp.zeros(o_ref.shape, o_ref.dtype)

    @pl.when(i < n_real)
    def _():
        _prompt_attn_block(i, q_ref, k_ref, vt_ref, qi3_ref, ki3_ref, wi_ref, o_ref,
                           s_ref, acc_ref, qz_ref, m_ref, l_ref, topk)


def _prompt_attn_block(i, q_ref, k_ref, vt_ref, qi3_ref, ki3_ref, wi_ref, o_ref,
                       s_ref, acc_ref, qz_ref, m_ref, l_ref, topk):
    qb, kc = Q_BLOCK, KEY_CHUNK
    f32, i32 = jnp.float32, jnp.int32
    last = i // (kc // qb)
    row_io = lax.broadcasted_iota(i32, (kc, qb), 0)
    lane_io = lax.broadcasted_iota(i32, (kc, qb), 1)
    causal_last = (row_io + last * kc) <= (lane_io + i * qb)

    def chunk_off(c):
        return pl.multiple_of(c * kc, kc)

    qi = qi3_ref[...]
    qstack = jnp.concatenate([qi[:, h * IDX_K:(h + 1) * IDX_K] for h in range(IDX_HEADS)], axis=0)
    w8 = wi_ref[0]

    def score_chunk(c, masked):
        st = _dot_nt(ki3_ref[0, pl.ds(chunk_off(c), kc), :], qstack)
        s = jnp.zeros((kc, qb), f32)
        for h in range(IDX_HEADS):
            s = s + w8[h:h + 1, :] * jnp.maximum(st[:, h * qb:(h + 1) * qb], 0.0)
        if masked:
            s = jnp.where(causal_last, s, -jnp.inf)
        s_ref[pl.ds(chunk_off(c), kc), :] = _sort_key(s)

    def score_body(c, z):
        score_chunk(c, False)
        return z

    lax.fori_loop(0, last, score_body, 0)
    score_chunk(last, True)

    def count_ge(cand):
        def body(c, acc):
            blk = s_ref[pl.ds(chunk_off(c), kc), :]
            hit = jnp.where(blk >= cand, 1, 0).astype(i32)
            return acc + jnp.sum(hit.reshape(kc // 8, 8, qb), axis=0)
        acc = lax.fori_loop(0, last + 1, body, jnp.zeros((8, qb), i32))
        return jnp.sum(acc, axis=0, keepdims=True)

    def bisect(step, carry):
        prefix, cnt_rej = carry
        cand = prefix ^ lax.shift_left(jnp.int32(1), 31 - step)
        cnt = count_ge(cand)
        ok = cnt >= topk
        return jnp.where(ok, cand, prefix), jnp.where(ok, cnt_rej, cnt)

    thr, cnt_gt = lax.fori_loop(0, 32, bisect, (jnp.full((1, qb), INT_MIN, i32), jnp.zeros((1, qb), i32)))
    need = (topk - cnt_gt).astype(f32)

    sq_r = lax.broadcasted_iota(i32, (kc, kc), 0)
    sq_c = lax.broadcasted_iota(i32, (kc, kc), 1)
    ltri = _bf(jnp.where(sq_c < sq_r, 1.0, 0.0))

    def bias_chunk(c, running, masked):
        key = s_ref[pl.ds(chunk_off(c), kc), :]
        eq = key == thr
        e = jnp.where(eq, 1.0, 0.0)
        rank = _dot(ltri, _bf(e)) + running
        sel = (key > thr) | (eq & (rank < need))
        if masked:
            sel = sel & causal_last
        s_ref[pl.ds(chunk_off(c), kc), :] = pltpu.bitcast(jnp.where(sel, 0.0, NEG_BIAS).astype(f32), i32)
        return running + jnp.sum(e, axis=0, keepdims=True)

    running = lax.fori_loop(0, last, lambda c, r: bias_chunk(c, r, False), jnp.zeros((1, qb), f32))
    bias_chunk(last, running, True)

    q = q_ref[...].astype(f32)
    low = lax.broadcasted_iota(i32, (qb, LANES), 1) < HEAD_DIM
    n_pair = N_HEADS // 2
    for p in range(n_pair):
        qp = q[:, p * LANES:(p + 1) * LANES]
        qz_ref[p] = _bf(jnp.concatenate([jnp.where(low, qp, 0.0), jnp.where(low, 0.0, qp)], axis=0))
    m_ref[...] = jnp.full(m_ref.shape, NEG_BIAS, f32)
    l_ref[...] = jnp.zeros(l_ref.shape, f32)
    acc_ref[...] = jnp.zeros(acc_ref.shape, f32)

    def attn_body(c, z):
        kb = k_ref[0, pl.ds(chunk_off(c), kc), :]
        vb = vt_ref[0, c]
        bias = pltpu.bitcast(s_ref[pl.ds(chunk_off(c), kc), :], f32)
        m_all, l_all = m_ref[...], l_ref[...]
        m_out, l_out = [], []
        for p in range(n_pair):
            lg = _dot_nt(kb[:, p * LANES:(p + 1) * LANES], qz_ref[p])
            for hh in range(2):
                h = 2 * p + hh
                x = lg[:, hh * qb:(hh + 1) * qb] + bias
                m_old = m_all[h:h + 1, :]
                m_new = jnp.maximum(m_old, jnp.max(x, axis=0, keepdims=True))
                alpha = jnp.exp2(m_old - m_new)
                pe = jnp.exp2(x - m_new)
                l_out.append(alpha * l_all[h:h + 1, :] + jnp.sum(pe, axis=0, keepdims=True))
                m_out.append(m_new)
                pv = _dot(vb[h * HEAD_DIM:(h + 1) * HEAD_DIM, :], _bf(pe))
                acc_ref[h * HEAD_DIM:(h + 1) * HEAD_DIM, :] = acc_ref[h * HEAD_DIM:(h + 1) * HEAD_DIM, :] * alpha + pv
        m_ref[...] = jnp.concatenate(m_out, axis=0)
        l_ref[...] = jnp.concatenate(l_out, axis=0)
        return z

    lax.fori_loop(0, last + 1, attn_body, 0)
    outs = [acc_ref[h * HEAD_DIM:(h + 1) * HEAD_DIM, :] / l_ref[h:h + 1, :] for h in range(N_HEADS)]
    o_ref[...] = _bf(jnp.concatenate(outs, axis=0).T)


def _prompt_attention(q, kb, vt, qi3, ki3, kiwt, *, n_seq, topk, n_real):
    rows = q.shape[0]
    tps = rows // n_seq
    nblk = tps // Q_BLOCK
    kb3 = kb.reshape(n_seq, tps, ATTN_DIM)
    ki33 = ki3.reshape(n_seq, tps, IDX_K)
    qrow = lambda b, i: (b * nblk + i, 0)
    resident = pl.Buffered(1)
    return pl.pallas_call(
        functools.partial(_prompt_attn_kernel, topk=topk, n_real=n_real),
        out_shape=jax.ShapeDtypeStruct((rows, ATTN_DIM), jnp.bfloat16),
        grid=(n_seq, nblk),
        in_specs=[
            pl.BlockSpec((Q_BLOCK, ATTN_DIM), qrow),
            pl.BlockSpec((1, tps, ATTN_DIM), lambda b, i: (b, 0, 0), pipeline_mode=resident),
            pl.BlockSpec((1, tps // KEY_CHUNK, ATTN_DIM, KEY_CHUNK), lambda b, i: (b, 0, 0, 0),
                         pipeline_mode=resident),
            pl.BlockSpec((Q_BLOCK, IDX_HEADS * IDX_K), qrow),
            pl.BlockSpec((1, tps, IDX_K), lambda b, i: (b, 0, 0), pipeline_mode=resident),
            pl.BlockSpec((1, 8, Q_BLOCK), lambda b, i: (b, IDX_DIM // 8, i)),
        ],
        out_specs=pl.BlockSpec((Q_BLOCK, ATTN_DIM), qrow),
        scratch_shapes=[
            pltpu.VMEM((tps, Q_BLOCK), jnp.int32),
            pltpu.VMEM((ATTN_DIM, Q_BLOCK), jnp.float32),
            pltpu.VMEM((N_HEADS // 2, 2 * Q_BLOCK, LANES), jnp.bfloat16),
            pltpu.VMEM((N_HEADS, Q_BLOCK), jnp.float32),
            pltpu.VMEM((N_HEADS, Q_BLOCK), jnp.float32),
        ],
        compiler_params=pltpu.CompilerParams(dimension_semantics=("parallel", "arbitrary"),
                                             vmem_limit_bytes=VMEM_LIMIT),
        name="prompt_attention",
    )(q, kb3, vt, qi3, ki33, kiwt)


def _decode_rows(n_pages):
    return -(-(n_pages + 1) // LANES) * LANES


def _decode_select_kernel(pt_ref, *refs, pages_per_step, n_pages, topk):
    del pt_ref
    page_refs = refs[:pages_per_step]
    qi_ref, kin_ref, wi_ref, out_ref, s_ref, thr_ref, need_ref = refs[pages_per_step:]
    b = pl.program_id(0)
    g = pl.program_id(1)
    n_seq = s_ref.shape[0]
    f32, i32 = jnp.float32, jnp.int32
    rows = s_ref.shape[1]

    qf = qi_ref[0]
    wcol = wi_ref[0]
    q_hi, q_lo = _split_hi_lo(qf)

    def scores_of(keys):
        k_hi, k_lo = _split_hi_lo(keys)
        st = _dot_nt(q_hi, k_hi) + _dot_nt(q_lo, k_hi) + _dot_nt(q_hi, k_lo)
        return jnp.sum(wcol * jnp.maximum(st, 0.0), axis=0, keepdims=True)

    for p in range(pages_per_step):
        s_ref[b, pl.ds(g * pages_per_step + p, 1), :] = _sort_key(scores_of(page_refs[p][0]))

    @pl.when(g == 0)
    def _():
        s_new = scores_of(kin_ref[0])
        lane = lax.broadcasted_iota(i32, (rows - n_pages, LANES), 1)
        sub = lax.broadcasted_iota(i32, (rows - n_pages, LANES), 0)
        tail = jnp.where((lane == 0) & (sub == 0), jnp.broadcast_to(s_new, (rows - n_pages, LANES)), -jnp.inf)
        s_ref[b, n_pages:rows, :] = _sort_key(tail)

    @pl.when((b == n_seq - 1) & (g == pl.num_programs(1) - 1))
    def _():
        keys = s_ref[...]

        def count_ge(cand):
            hit = jnp.where(keys >= cand, 1, 0).astype(i32)
            return jnp.sum(jnp.sum(hit, axis=1, keepdims=True), axis=2, keepdims=True)

        def bisect(step, carry):
            prefix, cnt_rej = carry
            cand = prefix ^ lax.shift_left(jnp.int32(1), 31 - step)
            cnt = count_ge(cand)
            ok = cnt >= topk
            return jnp.where(ok, cand, prefix), jnp.where(ok, cnt_rej, cnt)

        thr, cnt_gt = lax.fori_loop(0, 32, bisect, (jnp.full((n_seq, 1, 1), INT_MIN, i32),
                                                     jnp.zeros((n_seq, 1, 1), i32)))
        thr_ref[...] = jnp.broadcast_to(thr, thr_ref.shape)
        need_ref[...] = jnp.broadcast_to((topk - cnt_gt).astype(f32), need_ref.shape)

        a_io = lax.broadcasted_iota(i32, (LANES, LANES), 0)
        b_io = lax.broadcasted_iota(i32, (LANES, LANES), 1)
        upper = _bf(jnp.where(a_io < b_io, 1.0, 0.0))
        ones = jnp.ones((LANES, LANES), jnp.bfloat16)
        r_io = lax.broadcasted_iota(i32, (rows, rows), 0)
        c_io = lax.broadcasted_iota(i32, (rows, rows), 1)
        lower = _bf(jnp.where(c_io < r_io, 1.0, 0.0))
        row_id = lax.broadcasted_iota(i32, (rows, LANES), 0)
        slot_io = lax.broadcasted_iota(i32, (PAGE_SIZE, topk), 1).astype(f32)
        pick = lax.broadcasted_iota(i32, (16, PAGE_SIZE), 0)
        offs = lax.broadcasted_iota(i32, (16, PAGE_SIZE), 1).astype(f32)
        rt = _bf(jnp.where(pick == 0, offs, jnp.where(pick == 1, 1.0, 0.0)))

        def order_rank(flags):
            fb = _bf(flags)
            return _dot(fb, upper) + _dot(lower, _bf(_dot(fb, ones)))

        def per_seq(s, z):
            key = s_ref[s]
            t = thr_ref[s][0:1, :]
            eq = key == t
            tie_rank = order_rank(jnp.where(eq, 1.0, 0.0))
            sel = (key > t) | (eq & (tie_rank < need_ref[s][0:1, :]))
            sel_f = jnp.where(sel, 1.0, 0.0)
            slot = jnp.where(sel & (row_id < n_pages), order_rank(sel_f), -1.0)
            slot_t = slot.T
            off_sum = jnp.zeros((16, topk), f32)
            page_sum = jnp.zeros((16, topk), f32)
            for p in range(n_pages):
                hit = _bf(jnp.where(slot_t[:, p:p + 1] == slot_io, 1.0, 0.0))
                res = _dot(rt, hit)
                off_sum = off_sum + res
                page_sum = page_sum + float(p) * res
            idx = page_sum[1:2, :] * float(PAGE_SIZE) + off_sum[0:1, :]
            new_sel = jnp.broadcast_to(sel_f[n_pages:n_pages + 1, 0:1], (1, topk))
            out_ref[s] = jnp.concatenate([idx, new_sel, jnp.zeros((6, topk), f32)], axis=0).astype(i32)
            return z

        lax.fori_loop(0, n_seq, per_seq, 0)


def _decode_select(page_table, cache_kidx, qi_rows, ki_new, wi_rows, *, topk, pages_per_step):
    n_seq, n_pages = page_table.shape
    steps = n_pages // pages_per_step
    rows = _decode_rows(n_pages)

    def page_map(p):
        return lambda b, g, pt: (pt[b, g * pages_per_step + p], 0, 0)

    in_specs = [pl.BlockSpec((1, PAGE_SIZE, IDX_DIM), page_map(p)) for p in range(pages_per_step)]
    in_specs += [
        pl.BlockSpec((1, 16, IDX_DIM), lambda b, g, pt: (b, 0, 0)),
        pl.BlockSpec((1, PAGE_SIZE, IDX_DIM), lambda b, g, pt: (b, 0, 0)),
        pl.BlockSpec((1, 16, 1), lambda b, g, pt: (b, 0, 0)),
    ]
    grid_spec = pltpu.PrefetchScalarGridSpec(
        num_scalar_prefetch=1, grid=(n_seq, steps), in_specs=in_specs,
        out_specs=pl.BlockSpec((n_seq, 8, topk), lambda b, g, pt: (0, 0, 0)),
        scratch_shapes=[pltpu.VMEM((n_seq, rows, LANES), jnp.int32),
                        pltpu.VMEM((n_seq, 8, LANES), jnp.int32), pltpu.VMEM((n_seq, 8, LANES), jnp.float32)],
    )
    return pl.pallas_call(
        functools.partial(_decode_select_kernel, pages_per_step=pages_per_step, n_pages=n_pages, topk=topk),
        out_shape=jax.ShapeDtypeStruct((n_seq, 8, topk), jnp.int32),
        grid_spec=grid_spec,
        compiler_params=pltpu.CompilerParams(dimension_semantics=("arbitrary", "arbitrary"),
                                             vmem_limit_bytes=VMEM_LIMIT),
        name="decode_select",
    )(page_table, *([cache_kidx] * pages_per_step), qi_rows, ki_new, wi_rows)


def _decode_attn_kernel(pt_ref, idx_ref, new_ref, q_ref, kn_ref, vn_ref, k_hbm, v_hbm, o_ref, kbuf, vbuf, sem, *, topk):
    b = pl.program_id(0)
    n_seq = pl.num_programs(0)
    cur = b % 2
    f32 = jnp.float32

    def row_copies(seq, slot, r):
        key = idx_ref[seq, r]
        phys = pt_ref[seq, lax.shift_right_logical(key, PAGE_SIZE.bit_length() - 1)]
        off = key & (PAGE_SIZE - 1)
        return (pltpu.make_async_copy(k_hbm.at[phys, off], kbuf.at[slot, r], sem.at[slot]),
                pltpu.make_async_copy(v_hbm.at[phys, off], vbuf.at[slot, r], sem.at[slot]))

    def start_rows(seq, slot):
        def body(r, z):
            for cp in row_copies(seq, slot, r):
                cp.start()
            return z
        lax.fori_loop(0, topk, body, 0)

    @pl.when(b == 0)
    def _():
        start_rows(0, 0)

    @pl.when(b + 1 < n_seq)
    def _():
        start_rows(b + 1, 1 - cur)

    def wait_body(r, z):
        for cp in row_copies(b, cur, r):
            cp.wait()
        return z

    lax.fori_loop(0, topk, wait_body, 0)

    qv = q_ref[0]
    new_in = new_ref[b] > 0
    s = jnp.sum(kbuf[cur] * qv[None], axis=-1, keepdims=True)
    slot_id = lax.broadcasted_iota(jnp.int32, s.shape, 0)
    s = s + jnp.where((slot_id == topk - 1) & new_in, NEG_BIAS, 0.0)
    s_new = jnp.sum(kn_ref[0] * qv, axis=-1, keepdims=True) + jnp.where(new_in, 0.0, NEG_BIAS)
    m = jnp.maximum(jnp.max(s, axis=0), s_new)
    pe = jnp.exp2(s - m[None])
    pe_new = jnp.exp2(s_new - m)
    denom = jnp.sum(pe, axis=0) + pe_new
    o_ref[0] = ((jnp.sum(pe * vbuf[cur], axis=0) + pe_new * vn_ref[0]) / denom).astype(f32)


def _decode_attention(page_table, sel_idx, new_sel, q_rows, k_new, v_new, cache_k, cache_v, *, topk):
    n_seq = page_table.shape[0]
    per_seq = lambda b, pt, idx, new: (b, 0, 0)
    tile = pl.BlockSpec((1, N_HEADS, HEAD_DIM), per_seq)
    grid_spec = pltpu.PrefetchScalarGridSpec(
        num_scalar_prefetch=3, grid=(n_seq,),
        in_specs=[tile, tile, tile, pl.BlockSpec(memory_space=pl.ANY), pl.BlockSpec(memory_space=pl.ANY)],
        out_specs=tile,
        scratch_shapes=[pltpu.VMEM((2, topk, N_HEADS, HEAD_DIM), jnp.float32),
                        pltpu.VMEM((2, topk, N_HEADS, HEAD_DIM), jnp.float32),
                        pltpu.SemaphoreType.DMA((2,))],
    )
    return pl.pallas_call(
        functools.partial(_decode_attn_kernel, topk=topk),
        out_shape=jax.ShapeDtypeStruct((n_seq, N_HEADS, HEAD_DIM), jnp.float32),
        grid_spec=grid_spec,
        compiler_params=pltpu.CompilerParams(dimension_semantics=("arbitrary",), vmem_limit_bytes=VMEM_LIMIT),
        name="decode_attention",
    )(page_table, sel_idx, new_sel, q_rows, k_new, v_new, cache_k, cache_v)


def _shifted(u, halo, shift):
    rolled = pltpu.roll(u, shift, 0)
    row = lax.broadcasted_iota(jnp.int32, u.shape, 0)
    out = rolled
    for r in range(shift):
        out = jnp.where(row == r, halo[8 - shift + r:8 - shift + r + 1, :], out)
    return out


def _merge_kernel(*refs, sequential, tiles):
    if sequential:
        x_ref, a_ref, u_ref, cb_ref, sga_ref, sgc_ref, cw_ref, woa_ref, woc_ref, wm_ref, h_ref, halo_ref = refs
        u = u_ref[...]

        @pl.when(pl.program_id(0) % tiles == 0)
        def _():
            halo_ref[...] = jnp.zeros(halo_ref.shape, halo_ref.dtype)

        halo = halo_ref[...]
        u1 = _shifted(u, halo, 1)
        u2 = _shifted(u, halo, 2)
        halo_ref[...] = u[u.shape[0] - 8:, :]
    else:
        x_ref, a_ref, u_ref, cb_ref, sga_ref, sgc_ref, cw_ref, woa_ref, woc_ref, wm_ref, s0_ref, s1_ref, h_ref = refs
        u = u_ref[...]
        u2 = s0_ref[...]
        u1 = s1_ref[...]
    cw = cw_ref[...]
    conv = cb_ref[...] * (u2 * cw[0:1, :] + u1 * cw[1:2, :] + u * cw[2:3, :])
    mixed = sga_ref[...] * _dot(a_ref[...], woa_ref[...]) + sgc_ref[...] * _dot(_bf(conv), woc_ref[...])
    h_ref[...] = x_ref[...] + _dot(_bf(mixed), wm_ref[...])


def _merge(x, attn, u, cb, sga, sgc, conv_w, woa, woc, wm, states, *, tm, tiles):
    rows, d = x.shape
    sequential = states is None
    row = lambda i: (i, 0)
    const = lambda i: (0, 0)
    in_specs = [pl.BlockSpec((tm, d), row), pl.BlockSpec((tm, ATTN_DIM), row), pl.BlockSpec((tm, CONV_DIM), row),
                pl.BlockSpec((tm, CONV_DIM), row), pl.BlockSpec((tm, d), row), pl.BlockSpec((tm, d), row),
                pl.BlockSpec(conv_w.shape, const), pl.BlockSpec(woa.shape, const), pl.BlockSpec(woc.shape, const),
                pl.BlockSpec(wm.shape, const)]
    args = [x, attn, u, cb, sga, sgc, conv_w, woa, woc, wm]
    scratch = []
    if sequential:
        scratch = [pltpu.VMEM((8, CONV_DIM), jnp.float32)]
    else:
        in_specs += [pl.BlockSpec((tm, CONV_DIM), row), pl.BlockSpec((tm, CONV_DIM), row)]
        args += list(states)
    return pl.pallas_call(
        functools.partial(_merge_kernel, sequential=sequential, tiles=tiles),
        out_shape=jax.ShapeDtypeStruct((rows, d), jnp.float32),
        grid=(rows // tm,), in_specs=in_specs, out_specs=pl.BlockSpec((tm, d), row), scratch_shapes=scratch,
        compiler_params=pltpu.CompilerParams(dimension_semantics=("arbitrary",), vmem_limit_bytes=VMEM_LIMIT),
        name="merge",
    )(*args)


def _ffn_kernel(*refs, sequential, tiles, tail_off, tail_rows):
    if sequential:
        h_ref, nw_ref, wg_ref, wv_ref, wd_ref, cw_ref, cbias_ref, y_ref, gt_ref, hn_ref, halo_ref = refs
    else:
        h_ref, nw_ref, wg_ref, wv_ref, wd_ref, cw_ref, cbias_ref, s0_ref, s1_ref, y_ref, gt_ref, hn_ref = refs
    c = pl.program_id(1)

    @pl.when(c == 0)
    def _():
        hn_ref[...] = _bf(_rms(h_ref[...], nw_ref[...]))

    hn = hn_ref[...]
    g = _dot(hn, wg_ref[...])
    val = _dot(hn, wv_ref[...])
    gt_ref[0] = g[tail_off:tail_off + tail_rows, :]
    if sequential:
        @pl.when(pl.program_id(0) % tiles == 0)
        def _():
            halo_ref[c] = jnp.zeros(halo_ref.shape[1:], halo_ref.dtype)

        halo = halo_ref[c]
        g1 = _shifted(g, halo, 1)
        g2 = _shifted(g, halo, 2)
        halo_ref[c] = g[g.shape[0] - 8:, :]
    else:
        g2 = s0_ref[...]
        g1 = s1_ref[...]
    cw = cw_ref[...]
    gc = g2 * cw[0:1, :] + g1 * cw[1:2, :] + g * cw[2:3, :] + cbias_ref[...]
    act = gc * jax.nn.sigmoid(gc) * val
    part = _dot(_bf(act), wd_ref[...])

    @pl.when(c == 0)
    def _():
        y_ref[...] = h_ref[...] + part

    @pl.when(c != 0)
    def _():
        y_ref[...] = y_ref[...] + part


def _ffn(h, nw, wg, wv, wd, conv_w, conv_b, states, *, tm, tiles, n_chunk, tail_off, tail_rows):
    rows, d = h.shape
    dff = wg.shape[1]
    ch = dff // n_chunk
    sequential = states is None
    row = lambda i, c: (i, 0)
    in_specs = [pl.BlockSpec((tm, d), row), pl.BlockSpec((1, d), lambda i, c: (0, 0)),
                pl.BlockSpec((d, ch), lambda i, c: (0, c)), pl.BlockSpec((d, ch), lambda i, c: (0, c)),
                pl.BlockSpec((ch, d), lambda i, c: (c, 0)), pl.BlockSpec((conv_w.shape[0], ch), lambda i, c: (0, c)),
                pl.BlockSpec((1, ch), lambda i, c: (0, c))]
    args = [h, nw, wg, wv, wd, conv_w, conv_b]
    scratch = [pltpu.VMEM((tm, d), jnp.bfloat16)]
    if sequential:
        scratch.append(pltpu.VMEM((n_chunk, 8, ch), jnp.float32))
    else:
        in_specs += [pl.BlockSpec((tm, ch), lambda i, c: (i, c)), pl.BlockSpec((tm, ch), lambda i, c: (i, c))]
        args += list(states)
    n_tiles = rows // tm
    return pl.pallas_call(
        functools.partial(_ffn_kernel, sequential=sequential, tiles=tiles, tail_off=tail_off, tail_rows=tail_rows),
        out_shape=(jax.ShapeDtypeStruct((rows, d), jnp.float32),
                   jax.ShapeDtypeStruct((n_tiles, tail_rows, dff), jnp.float32)),
        grid=(n_tiles, n_chunk), in_specs=in_specs,
        out_specs=(pl.BlockSpec((tm, d), row), pl.BlockSpec((1, tail_rows, ch), lambda i, c: (i, 0, c))),
        scratch_shapes=scratch,
        compiler_params=pltpu.CompilerParams(dimension_semantics=("arbitrary", "arbitrary"),
                                             vmem_limit_bytes=VMEM_LIMIT),
        name="ffn",
    )(*args)


def _rope_tables(pos):
    half = HEAD_DIM // 2
    freqs = ROPE_THETA ** (-jnp.arange(half, dtype=jnp.float32) / half)
    ang = pos.astype(jnp.float32)[:, None] * freqs[None, :]
    cos, sin = jnp.cos(ang), jnp.sin(ang)
    reps = LANES // HEAD_DIM
    return jnp.concatenate([cos, cos] * reps, axis=1), jnp.concatenate([-sin, sin] * reps, axis=1)


def kernel(x_prompt, x_sample, cache_k, cache_v, cache_kidx, state_conv, state_ffn_conv, page_table, meta_tokens, attn_norm_w, w_in, q_norm_w, k_norm_w, conv_w, w_o_attn, w_o_conv, w_merge, ffn_norm_w, w_up, ffn_conv_w, ffn_conv_b, w_down):
    f32 = jnp.float32
    n_b, seq, d = x_prompt.shape
    n_db, n_ds = x_sample.shape[0], x_sample.shape[1]
    depth = w_in.shape[0]
    assert depth == 1 and n_ds == 1
    n_pages = page_table.shape[1]
    past = n_pages * PAGE_SIZE
    t_len = seq + N_META
    assert t_len % 8 == 0
    topk_p = min(TOPK_MAX, seq // 4)
    topk_s = min(TOPK_MAX, (past + n_ds) // 4)
    n_real = -(-t_len // Q_BLOCK)
    tm = KEY_CHUNK
    tiles = -(-t_len // tm)
    tp = tiles * tm
    dff = w_down.shape[1]
    n_chunk = 2 if dff % 256 == 0 else 1

    l = 0
    o_idx = 3 * ATTN_DIM
    n_idx = IDX_HEADS * IDX_DIM + IDX_DIM + IDX_HEADS
    o_conv = o_idx + n_idx
    wqkv = _bf(w_in[l][:, :o_idx])
    w_idx = jnp.pad(w_in[l][:, o_idx:o_conv], ((0, 0), (0, 3 * LANES - n_idx)))
    whi, wlo = _split_hi_lo(w_idx)
    w_b = _bf(w_in[l][:, o_conv:])
    woa, woc, wmg = _bf(w_o_attn[l]), _bf(w_o_conv[l]), _bf(w_merge[l])
    wg, wv, wd = _bf(w_up[l][:, :dff]), _bf(w_up[l][:, dff:]), _bf(w_down[l])
    nw = attn_norm_w[l][None, :]
    fnw = ffn_norm_w[l][None, :]
    qnw = jnp.tile(q_norm_w[l], N_HEADS)[None, :]
    knw = jnp.tile(k_norm_w[l], N_HEADS)[None, :]
    hid = np.arange(ATTN_DIM) // HEAD_DIM
    gmat = jnp.asarray(hid[:, None] == hid[None, :], jnp.bfloat16)
    fcb = ffn_conv_b[l][None, :]

    hp = jnp.concatenate([jnp.broadcast_to(meta_tokens.astype(f32)[None], (n_b, N_META, d)), x_prompt,
                          jnp.zeros((n_b, tp - t_len, d), f32)], axis=1).reshape(n_b * tp, d)
    cos_p, sin_p = _rope_tables(jnp.arange(tp, dtype=jnp.int32))
    q, kf, kb, vf, vt, qi3, kif, ki3, kiwt = _inproj_a(hp, nw, wqkv, whi, wlo, qnw, knw, gmat, cos_p, sin_p,
                                                        n_seq=n_b, tm=tm, emit_vt=True)
    u, cb, sga, sgc = _inproj_b(hp, nw, w_b, tm=tm)
    attn = _prompt_attention(q, kb, vt, qi3, ki3, kiwt, n_seq=n_b, topk=topk_p, n_real=n_real)
    h = _merge(hp, attn, u, cb, sga, sgc, conv_w[l], woa, woc, wmg, None, tm=tm, tiles=tiles)
    tail_tile, tail_off = (t_len - 8) // tm, (t_len - 8) % tm
    y, g_tail = _ffn(h, fnw, wg, wv, wd, ffn_conv_w[l], fcb, None, tm=tm, tiles=tiles, n_chunk=n_chunk,
                     tail_off=tail_off, tail_rows=8)

    y_prompt = y.reshape(n_b, tp, d)[:, N_META:t_len]
    new_k_p = kf.reshape(n_b, tp, N_HEADS, HEAD_DIM)[:, :t_len][None]
    new_v_p = vf.reshape(n_b, tp, N_HEADS, HEAD_DIM)[:, :t_len][None]
    new_ki_p = kif.reshape(n_b, tp, IDX_DIM)[:, :t_len][None]
    new_conv_p = u.reshape(n_b, tp, CONV_DIM)[:, t_len - 2:t_len][None]
    new_ffn_p = g_tail.reshape(n_b, tiles, 8, dff)[:, tail_tile, 6:8][None]

    xs = x_sample.reshape(n_db, d)
    n_pad = -(-n_db // LANES) * LANES
    xs_pad = jnp.pad(xs, ((0, n_pad - n_db), (0, 0)))
    cos_s, sin_s = _rope_tables(jnp.full((n_pad,), past, jnp.int32))
    qs, kfs, _, vfs, _, qi3s, kifs, _, kiwts = _inproj_a(xs_pad, nw, wqkv, whi, wlo, qnw, knw, gmat, cos_s, sin_s,
                                                          n_seq=1, tm=n_pad, emit_vt=False)
    qs, kfs, vfs, kifs = qs[:n_db], kfs[:n_db], vfs[:n_db], kifs[:n_db]
    us, cbs, sgas, sgcs = _inproj_b(xs, nw, w_b, tm=n_db)
    qi4 = qi3s[:n_db].astype(f32).reshape(n_db, IDX_HEADS, IDX_K // IDX_DIM, IDX_DIM)
    qi_rows = jnp.pad(qi4[:, :, 0] + qi4[:, :, 1], ((0, 0), (0, 16 - IDX_HEADS), (0, 0)))
    wi_rows = jnp.pad(kiwts[0, IDX_DIM:IDX_DIM + IDX_HEADS, :n_db].T, ((0, 0), (0, 16 - IDX_HEADS)))[:, :, None]
    ki_new = jnp.broadcast_to(kifs[:, None, :], (n_db, PAGE_SIZE, IDX_DIM))
    pps_sel = 16 if n_pages % 16 == 0 else 1
    picked = _decode_select(page_table, cache_kidx[l], qi_rows, ki_new, wi_rows, topk=topk_s, pages_per_step=pps_sel)
    as_tiles = lambda a: a.astype(f32).reshape(n_db, N_HEADS, HEAD_DIM)
    attn_s = _decode_attention(page_table, picked[:, 0, :], picked[:, 1, 0], as_tiles(qs), as_tiles(kfs),
                               as_tiles(vfs), cache_k[l], cache_v[l], topk=topk_s)
    sc = state_conv[l]
    hs = _merge(xs, _bf(attn_s.reshape(n_db, ATTN_DIM)), us, cbs, sgas, sgcs, conv_w[l], woa, woc, wmg,
                (sc[:, 0], sc[:, 1]), tm=n_db, tiles=1)
    sf = state_ffn_conv[l]
    ys, g_s = _ffn(hs, fnw, wg, wv, wd, ffn_conv_w[l], fcb, (sf[:, 0], sf[:, 1]), tm=n_db, tiles=1, n_chunk=n_chunk,
                   tail_off=0, tail_rows=n_db)

    y_sample = ys.reshape(n_db, n_ds, d)
    new_k_s = kfs.reshape(1, n_db, n_ds, N_HEADS, HEAD_DIM)
    new_v_s = vfs.reshape(1, n_db, n_ds, N_HEADS, HEAD_DIM)
    new_ki_s = kifs.reshape(1, n_db, n_ds, IDX_DIM)
    new_conv_s = jnp.stack([sc[:, 1], us], axis=1)[None]
    new_ffn_s = jnp.stack([sf[:, 1], g_s[0]], axis=1)[None]
    return (y_prompt, y_sample, new_k_p, new_v_p, new_ki_p, new_conv_p, new_ffn_p,
            new_k_s, new_v_s, new_ki_s, new_conv_s, new_ffn_s)
```

```python
import functools

import jax
import jax.numpy as jnp
import numpy as np
from jax import lax
from jax.experimental import pallas as pl
from jax.experimental.pallas import tpu as pltpu

N_META = 16
N_HEADS = 8
HEAD_DIM = 64
ATTN_DIM = N_HEADS * HEAD_DIM
IDX_HEADS = 4
IDX_DIM = 64
TOPK_MAX = 256
CONV_DIM = 512
PAGE_SIZE = 128
ROPE_THETA = 10000.0
EPS = 1e-6
Q_BLOCK = 128
KEY_CHUNK = 512
LANES = 128
LOG2_E = 1.4426950408889634
IDX_K = 256
NEG_BIAS = -1e30
INT_MIN = -(2 ** 31)
VMEM_LIMIT = 56 * 1024 * 1024

_NT = (((1,), (1,)), ((), ()))


def _bf(x):
    return x.astype(jnp.bfloat16)


def _split_hi_lo(x):
    hi = _bf(x)
    lo = _bf(x - hi.astype(jnp.float32))
    return hi, lo


def _dot(a, b):
    return jnp.dot(a, b, preferred_element_type=jnp.float32)


def _dot_nt(a, b):
    return lax.dot_general(a, b, _NT, preferred_element_type=jnp.float32)


def _rms(x, w):
    ms = jnp.mean(x * x, axis=-1, keepdims=True)
    return x * lax.rsqrt(ms + EPS) * w


def _rope_slab(x, cos, sin):
    lane = lax.broadcasted_iota(jnp.int32, x.shape, 1)
    first_half = (lane & (HEAD_DIM - 1)) < (HEAD_DIM // 2)
    swapped = jnp.where(first_half, pltpu.roll(x, LANES - HEAD_DIM // 2, 1), pltpu.roll(x, HEAD_DIM // 2, 1))
    return x * cos + swapped * sin


def _sort_key(s):
    s = jnp.where(s == 0.0, 0.0, s)
    b = pltpu.bitcast(s, jnp.int32)
    return jnp.where(b >= 0, b, b ^ jnp.int32(0x7FFFFFFF))


def _inproj_a_kernel(x_ref, nw_ref, wqkv_ref, wvt_ref, whi_ref, wlo_ref, qnw_ref, knw_ref, g_ref, cos_ref, sin_ref,
                     q_ref, kf_ref, kb_ref, vf_ref, vt_ref, qi3_ref, kif_ref, ki3_ref, kiwt_ref, *, emit_vt):
    xn = _rms(x_ref[...], nw_ref[...])
    xhi, xlo = _split_hi_lo(xn)
    cos = cos_ref[...]
    sin = sin_ref[...]
    gmat = g_ref[...]
    n_slab = ATTN_DIM // LANES

    def qk_path(col0, nw):
        y = _dot(xhi, wqkv_ref[:, col0:col0 + ATTN_DIM])
        ss = _dot(_bf(y * y), gmat)
        y = y * lax.rsqrt(ss * (1.0 / HEAD_DIM) + EPS) * nw
        return jnp.concatenate([_rope_slab(y[:, s * LANES:(s + 1) * LANES], cos, sin) for s in range(n_slab)], axis=1)

    q = qk_path(0, qnw_ref[...])
    q_ref[...] = _bf(q * (HEAD_DIM ** -0.5 * LOG2_E))
    k = qk_path(ATTN_DIM, knw_ref[...])
    kb_ref[...] = _bf(k)
    if emit_vt:
        kf_ref[0] = k.T
        v_t = _dot_nt(wvt_ref[...], xhi)
        vf_ref[0] = v_t
        vt_ref[0, 0] = _bf(v_t)
    else:
        kf_ref[...] = k
        vf_ref[...] = _dot(xhi, wqkv_ref[:, 2 * ATTN_DIM:3 * ATTN_DIM])
        vt_ref[...] = jnp.zeros(vt_ref.shape, vt_ref.dtype)

    idx = _dot(xhi, whi_ref[...]) + _dot(xlo, whi_ref[...]) + _dot(xhi, wlo_ref[...])
    lane = lax.broadcasted_iota(jnp.int32, (idx.shape[0], LANES), 1)
    low = lane < IDX_DIM
    zero = jnp.zeros((idx.shape[0], LANES), jnp.float32)

    def hi_lo_f32(y):
        hi = _bf(y).astype(jnp.float32)
        return hi, y - hi

    pieces = []
    for s in range(IDX_HEADS * IDX_DIM // LANES):
        slab = _rope_slab(idx[:, s * LANES:(s + 1) * LANES], cos, sin)
        hi, lo = hi_lo_f32(slab)
        hi_r, lo_r = hi_lo_f32(pltpu.roll(slab, IDX_DIM, 1))
        pieces += [jnp.where(low, hi, lo_r), jnp.where(low, hi, zero),
                   jnp.where(low, hi_r, lo), jnp.where(low, hi_r, zero)]
    qi3_ref[...] = _bf(jnp.concatenate(pieces, axis=1))
    c3 = idx[:, IDX_HEADS * IDX_DIM:IDX_HEADS * IDX_DIM + LANES]
    ki = _rope_slab(c3, cos, sin)
    if emit_vt:
        kif_ref[0] = ki.T[:IDX_DIM, :]
    else:
        kif_ref[...] = ki[:, :IDX_DIM]
    hi, lo = hi_lo_f32(ki)
    hi_r = pltpu.roll(hi, IDX_DIM, 1)
    ki3_ref[...] = _bf(jnp.concatenate([jnp.where(low, hi, hi_r), jnp.where(low, lo, zero)], axis=1))
    kiwt_ref[0] = c3.T


def _inproj_a(x, nw, wqkv, wvt, whi, wlo, qnw, knw, gmat, cos, sin, *, n_seq, tm, emit_vt, t_out=None):
    rows = x.shape[0]
    d = x.shape[1]
    tps = rows // n_seq
    tiles = tps // tm
    grid = (rows // tm,)
    row = lambda i: (i, 0)
    const = lambda i: (0, 0)
    tab = lambda i: (i % tiles, 0)
    seq_tile = lambda i: (i // tiles, 0, i % tiles)
    f32, bf16 = jnp.float32, jnp.bfloat16
    vt_shape = (n_seq, tiles, ATTN_DIM, tm) if emit_vt else (rows // tm, 1, 16, LANES)
    vt_block = (1, 1, ATTN_DIM, tm) if emit_vt else (1, 1, 16, LANES)
    vt_map = (lambda i: (i // tiles, i % tiles, 0, 0)) if emit_vt else (lambda i: (i, 0, 0, 0))
    wide = jax.ShapeDtypeStruct((n_seq, ATTN_DIM, t_out) if emit_vt else (rows, ATTN_DIM), f32)
    wide_spec = pl.BlockSpec((1, ATTN_DIM, tm), seq_tile) if emit_vt else pl.BlockSpec((tm, ATTN_DIM), row)
    narrow = jax.ShapeDtypeStruct((n_seq, IDX_DIM, t_out) if emit_vt else (rows, IDX_DIM), f32)
    narrow_spec = pl.BlockSpec((1, IDX_DIM, tm), seq_tile) if emit_vt else pl.BlockSpec((tm, IDX_DIM), row)
    out_shape = (
        jax.ShapeDtypeStruct((rows, ATTN_DIM), bf16),
        wide,
        jax.ShapeDtypeStruct((rows, ATTN_DIM), bf16),
        wide,
        jax.ShapeDtypeStruct(vt_shape, bf16),
        jax.ShapeDtypeStruct((rows, IDX_HEADS * IDX_K), bf16),
        narrow,
        jax.ShapeDtypeStruct((rows, IDX_K), bf16),
        jax.ShapeDtypeStruct((n_seq, LANES, tps), f32),
    )
    out_specs = (
        pl.BlockSpec((tm, ATTN_DIM), row), wide_spec, pl.BlockSpec((tm, ATTN_DIM), row),
        wide_spec, pl.BlockSpec(vt_block, vt_map),
        pl.BlockSpec((tm, IDX_HEADS * IDX_K), row), narrow_spec, pl.BlockSpec((tm, IDX_K), row),
        pl.BlockSpec((1, LANES, tm), seq_tile),
    )
    in_specs = [
        pl.BlockSpec((tm, d), row), pl.BlockSpec((1, d), const), pl.BlockSpec(wqkv.shape, const),
        pl.BlockSpec(wvt.shape, const), pl.BlockSpec(whi.shape, const), pl.BlockSpec(wlo.shape, const),
        pl.BlockSpec((1, ATTN_DIM), const), pl.BlockSpec((1, ATTN_DIM), const), pl.BlockSpec(gmat.shape, const),
        pl.BlockSpec((tm, LANES), tab), pl.BlockSpec((tm, LANES), tab),
    ]
    return pl.pallas_call(
        functools.partial(_inproj_a_kernel, emit_vt=emit_vt),
        out_shape=out_shape, grid=grid, in_specs=in_specs, out_specs=out_specs,
        compiler_params=pltpu.CompilerParams(dimension_semantics=("parallel",), vmem_limit_bytes=VMEM_LIMIT),
        name="inproj_a",
    )(x, nw, wqkv, wvt, whi, wlo, qnw, knw, gmat, cos, sin)


def _inproj_b_kernel(x_ref, nw_ref, w_ref, u_ref, cb_ref, sga_ref, sgc_ref):
    xb = _bf(_rms(x_ref[...], nw_ref[...]))
    d = x_ref.shape[1]
    cx = _dot(xb, w_ref[:, 0:CONV_DIM])
    cb_ref[...] = _dot(xb, w_ref[:, CONV_DIM:2 * CONV_DIM])
    cc = _dot(xb, w_ref[:, 2 * CONV_DIM:3 * CONV_DIM])
    u_ref[...] = cc * cx
    sga_ref[...] = jax.nn.sigmoid(_dot(xb, w_ref[:, 3 * CONV_DIM:3 * CONV_DIM + d]))
    sgc_ref[...] = jax.nn.sigmoid(_dot(xb, w_ref[:, 3 * CONV_DIM + d:3 * CONV_DIM + 2 * d]))


def _inproj_b(x, nw, w, *, tm):
    rows, d = x.shape
    row = lambda i: (i, 0)
    const = lambda i: (0, 0)
    f32 = jnp.float32
    return pl.pallas_call(
        _inproj_b_kernel,
        out_shape=(jax.ShapeDtypeStruct((rows, CONV_DIM), f32), jax.ShapeDtypeStruct((rows, CONV_DIM), f32),
                   jax.ShapeDtypeStruct((rows, d), f32), jax.ShapeDtypeStruct((rows, d), f32)),
        grid=(rows // tm,),
        in_specs=[pl.BlockSpec((tm, d), row), pl.BlockSpec((1, d), const), pl.BlockSpec(w.shape, const)],
        out_specs=(pl.BlockSpec((tm, CONV_DIM), row), pl.BlockSpec((tm, CONV_DIM), row),
                   pl.BlockSpec((tm, d), row), pl.BlockSpec((tm, d), row)),
        compiler_params=pltpu.CompilerParams(dimension_semantics=("parallel",), vmem_limit_bytes=VMEM_LIMIT),
        name="inproj_b",
    )(x, nw, w)


def _prompt_attn_kernel(q_ref, k_ref, vt_ref, qi3_ref, ki3_ref, wi_ref, o_ref,
                        s_ref, acc_ref, qz_ref, m_ref, l_ref, lg_ref,*, topk, n_real):
    i = pl.program_id(1)

    @pl.when(i >= n_real)
    def _():
        o_ref[...] = jnp.zeros(o_ref.shape, o_ref.dtype)

    @pl.when(i < n_real)
    def _():
        _prompt_attn_block(i, q_ref, k_ref, vt_ref, qi3_ref, ki3_ref, wi_ref, o_ref,
                           s_ref, acc_ref, qz_ref, m_ref, l_ref, lg_ref,topk)


def _prompt_attn_block(i, q_ref, k_ref, vt_ref, qi3_ref, ki3_ref, wi_ref, o_ref,
                       s_ref, acc_ref, qz_ref, m_ref, l_ref, lg_ref,topk):
    qb, kc = Q_BLOCK, KEY_CHUNK
    f32, i32 = jnp.float32, jnp.int32
    last = i // (kc // qb)
    row_io = lax.broadcasted_iota(i32, (kc, qb), 0)
    lane_io = lax.broadcasted_iota(i32, (kc, qb), 1)
    causal_last = (row_io + last * kc) <= (lane_io + i * qb)

    def chunk_off(c):
        return pl.multiple_of(c * kc, kc)

    qi = qi3_ref[...]
    qstack = jnp.concatenate([qi[:, h * IDX_K:(h + 1) * IDX_K] for h in range(IDX_HEADS)], axis=0)
    w8 = wi_ref[0]

    def score_chunk(c, masked):
        st = _dot_nt(ki3_ref[0, pl.ds(chunk_off(c), kc), :], qstack)
        s = jnp.zeros((kc, qb), f32)
        for h in range(IDX_HEADS):
            s = s + w8[h:h + 1, :] * jnp.maximum(st[:, h * qb:(h + 1) * qb], 0.0)
        if masked:
            s = jnp.where(causal_last, s, -jnp.inf)
        s_ref[pl.ds(chunk_off(c), kc), :] = _sort_key(s)

    def score_body(c, z):
        score_chunk(c, False)
        return z

    lax.fori_loop(0, last, score_body, 0)
    score_chunk(last, True)

    def count_ge(cand):
        def body(c, acc):
            blk = s_ref[pl.ds(chunk_off(c), kc), :]
            hit = jnp.where(blk >= cand, 1, 0).astype(i32)
            return acc + jnp.sum(hit.reshape(kc // 8, 8, qb), axis=0)
        acc = lax.fori_loop(0, last + 1, body, jnp.zeros((8, qb), i32))
        return jnp.sum(acc, axis=0, keepdims=True)

    def bisect(step, carry):
        prefix, cnt_rej = carry
        cand = prefix ^ lax.shift_left(jnp.int32(1), 31 - step)
        cnt = count_ge(cand)
        ok = cnt >= topk
        return jnp.where(ok, cand, prefix), jnp.where(ok, cnt_rej, cnt)

    thr, cnt_gt = lax.fori_loop(0, 32, bisect, (jnp.full((1, qb), INT_MIN, i32), jnp.zeros((1, qb), i32)))
    need = (topk - cnt_gt).astype(f32)

    sq_r = lax.broadcasted_iota(i32, (qb, qb), 0)
    sq_c = lax.broadcasted_iota(i32, (qb, qb), 1)
    ltri = _bf(jnp.where(sq_c < sq_r, 1.0, 0.0))

    def bias_chunk(c, running, masked):
        key = s_ref[pl.ds(chunk_off(c), kc), :]
        eq = key == thr
        e = jnp.where(eq, 1.0, 0.0)
        ranks = []
        for blk in range(kc // qb):
            eb = e[blk * qb:(blk + 1) * qb, :]
            ranks.append(_dot(ltri, _bf(eb)) + running)
            running = running + jnp.sum(eb, axis=0, keepdims=True)
        sel = (key > thr) | (eq & (jnp.concatenate(ranks, axis=0) < need))
        if masked:
            sel = sel & causal_last
        s_ref[pl.ds(chunk_off(c), kc), :] = pltpu.bitcast(jnp.where(sel, 0.0, NEG_BIAS).astype(f32), i32)
        return running

    running = lax.fori_loop(0, last, lambda c, r: bias_chunk(c, r, False), jnp.zeros((1, qb), f32))
    bias_chunk(last, running, True)

    q = q_ref[...].astype(f32)
    low = lax.broadcasted_iota(i32, (qb, LANES), 1) < HEAD_DIM
    n_pair = N_HEADS // 2
    for p in range(n_pair):
        qp = q[:, p * LANES:(p + 1) * LANES]
        qz_ref[p] = _bf(jnp.concatenate([jnp.where(low, qp, 0.0), jnp.where(low, 0.0, qp)], axis=0))
    m_ref[...] = jnp.full(m_ref.shape, NEG_BIAS, f32)
    l_ref[...] = jnp.zeros(l_ref.shape, f32)
    acc_ref[...] = jnp.zeros(acc_ref.shape, f32)

    def logits_into(c, buf):
        kb = k_ref[0, pl.ds(chunk_off(c), kc), :]
        for p in range(n_pair):
            lg_ref[buf, p] = _dot_nt(kb[:, p * LANES:(p + 1) * LANES], qz_ref[p])

    logits_into(0, 0)

    def attn_body(c, z):
        buf = c % 2
        vb = vt_ref[0, c]
        bias = pltpu.bitcast(s_ref[pl.ds(chunk_off(c), kc), :], f32)
        m_all, l_all = m_ref[...], l_ref[...]
        m_out, l_out = [], []
        for p in range(n_pair):
            lg = lg_ref[buf, p]
            for hh in range(2):
                h = 2 * p + hh
                x = lg[:, hh * qb:(hh + 1) * qb] + bias
                m_old = m_all[h:h + 1, :]
                m_new = jnp.maximum(m_old, jnp.max(x, axis=0, keepdims=True))
                alpha = jnp.exp2(m_old - m_new)
                pe = jnp.exp2(x - m_new)
                l_out.append(alpha * l_all[h:h + 1, :] + jnp.sum(pe, axis=0, keepdims=True))
                m_out.append(m_new)
                pv = _dot(vb[h * HEAD_DIM:(h + 1) * HEAD_DIM, :], _bf(pe))
                acc_ref[h * HEAD_DIM:(h + 1) * HEAD_DIM, :] = acc_ref[h * HEAD_DIM:(h + 1) * HEAD_DIM, :] * alpha + pv
        m_ref[...] = jnp.concatenate(m_out, axis=0)
        l_ref[...] = jnp.concatenate(l_out, axis=0)
        logits_into(jnp.minimum(c + 1, last), 1 - buf)
        return z

    lax.fori_loop(0, last + 1, attn_body, 0)
    outs = [acc_ref[h * HEAD_DIM:(h + 1) * HEAD_DIM, :] / l_ref[h:h + 1, :] for h in range(N_HEADS)]
    o_ref[...] = _bf(jnp.concatenate(outs, axis=0).T)


def _prompt_attention(q, kb, vt, qi3, ki3, kiwt, *, n_seq, topk, n_real):
    rows = q.shape[0]
    tps = rows // n_seq
    nblk = tps // Q_BLOCK
    kb3 = kb.reshape(n_seq, tps, ATTN_DIM)
    ki33 = ki3.reshape(n_seq, tps, IDX_K)
    qrow = lambda b, i: (b * nblk + i, 0)
    resident = pl.Buffered(1)
    return pl.pallas_call(
        functools.partial(_prompt_attn_kernel, topk=topk, n_real=n_real),
        out_shape=jax.ShapeDtypeStruct((rows, ATTN_DIM), jnp.bfloat16),
        grid=(n_seq, nblk),
        in_specs=[
            pl.BlockSpec((Q_BLOCK, ATTN_DIM), qrow),
            pl.BlockSpec((1, tps, ATTN_DIM), lambda b, i: (b, 0, 0), pipeline_mode=resident),
            pl.BlockSpec((1, tps // KEY_CHUNK, ATTN_DIM, KEY_CHUNK), lambda b, i: (b, 0, 0, 0),
                         pipeline_mode=resident),
            pl.BlockSpec((Q_BLOCK, IDX_HEADS * IDX_K), qrow),
            pl.BlockSpec((1, tps, IDX_K), lambda b, i: (b, 0, 0), pipeline_mode=resident),
            pl.BlockSpec((1, 8, Q_BLOCK), lambda b, i: (b, IDX_DIM // 8, i)),
        ],
        out_specs=pl.BlockSpec((Q_BLOCK, ATTN_DIM), qrow),
        scratch_shapes=[
            pltpu.VMEM((tps, Q_BLOCK), jnp.int32),
            pltpu.VMEM((ATTN_DIM, Q_BLOCK), jnp.float32),
            pltpu.VMEM((N_HEADS // 2, 2 * Q_BLOCK, LANES), jnp.bfloat16),
            pltpu.VMEM((N_HEADS, Q_BLOCK), jnp.float32),
            pltpu.VMEM((N_HEADS, Q_BLOCK), jnp.float32),
            pltpu.VMEM((2, N_HEADS // 2, KEY_CHUNK, 2 * Q_BLOCK), jnp.float32),
        ],
        compiler_params=pltpu.CompilerParams(dimension_semantics=("parallel", "arbitrary"),
                                             vmem_limit_bytes=VMEM_LIMIT),
        name="prompt_attention",
    )(q, kb3, vt, qi3, ki33, kiwt)


def _decode_rows(n_pages):
    return -(-(n_pages + 1) // LANES) * LANES


def _decode_select_kernel(pt_ref, *refs, pages_per_step, n_pages, topk):
    del pt_ref
    page_refs = refs[:pages_per_step]
    qi_ref, kin_ref, wi_ref, bias_ref, s_ref = refs[pages_per_step:]
    b = pl.program_id(0)
    g = pl.program_id(1)
    n_seq = s_ref.shape[0]
    f32, i32 = jnp.float32, jnp.int32
    rows = s_ref.shape[1]

    qf = qi_ref[0]
    wcol = wi_ref[0]
    q_hi, q_lo = _split_hi_lo(qf)

    def scores_of(keys_t):
        k_hi, k_lo = _split_hi_lo(keys_t)
        st = _dot(q_hi, k_hi) + _dot(q_lo, k_hi) + _dot(q_hi, k_lo)
        return jnp.sum(wcol * jnp.maximum(st, 0.0), axis=0, keepdims=True)

    for p in range(pages_per_step):
        s_ref[b, pl.ds(g * pages_per_step + p, 1), :] = _sort_key(scores_of(page_refs[p][0]))

    @pl.when(g == 0)
    def _():
        s_new = scores_of(kin_ref[0])
        lane = lax.broadcasted_iota(i32, (rows - n_pages, LANES), 1)
        sub = lax.broadcasted_iota(i32, (rows - n_pages, LANES), 0)
        tail = jnp.where((lane == 0) & (sub == 0), jnp.broadcast_to(s_new, (rows - n_pages, LANES)), -jnp.inf)
        s_ref[b, n_pages:rows, :] = _sort_key(tail)

    @pl.when((b == n_seq - 1) & (g == pl.num_programs(1) - 1))
    def _():
        keys = s_ref[...]

        def count_ge(cand):
            hit = jnp.where(keys >= cand, 1, 0).astype(i32)
            return jnp.sum(jnp.sum(hit, axis=1, keepdims=True), axis=2, keepdims=True)

        def bisect(step, carry):
            prefix, cnt_rej = carry
            cand = prefix ^ lax.shift_left(jnp.int32(1), 31 - step)
            cnt = count_ge(cand)
            ok = cnt >= topk
            return jnp.where(ok, cand, prefix), jnp.where(ok, cnt_rej, cnt)

        thr, cnt_gt = lax.fori_loop(0, 32, bisect, (jnp.full((n_seq, 1, 1), INT_MIN, i32),
                                                     jnp.zeros((n_seq, 1, 1), i32)))
        need = (topk - cnt_gt).astype(f32)
        eq = keys == thr
        e2 = _bf(jnp.where(eq, 1.0, 0.0).reshape(n_seq * rows, LANES))
        a_io = lax.broadcasted_iota(i32, (LANES, LANES), 0)
        b_io = lax.broadcasted_iota(i32, (LANES, LANES), 1)
        upper = _bf(jnp.where(a_io < b_io, 1.0, 0.0))
        in_row = _dot(e2, upper).reshape(n_seq, rows, LANES)
        row_tot = _dot(e2, jnp.ones((LANES, LANES), jnp.bfloat16)).reshape(n_seq, rows, LANES)
        r_io = lax.broadcasted_iota(i32, (rows, rows), 0)
        c_io = lax.broadcasted_iota(i32, (rows, rows), 1)
        lower = _bf(jnp.where(c_io < r_io, 1.0, 0.0))
        for s in range(n_seq):
            rank = in_row[s] + _dot(lower, _bf(row_tot[s]))
            sel = (keys[s] > thr[s]) | (eq[s] & (rank < need[s]))
            bias_ref[s] = jnp.where(sel, 0.0, NEG_BIAS).astype(f32)


def _decode_select(page_table, cache_kidx, qi_rows, ki_new, wi_rows, *, topk, pages_per_step):
    n_seq, n_pages = page_table.shape
    steps = n_pages // pages_per_step
    rows = _decode_rows(n_pages)

    def page_map(p):
        return lambda b, g, pt: (pt[b, g * pages_per_step + p], 0, 0)

    in_specs = [pl.BlockSpec((1, IDX_DIM, PAGE_SIZE), page_map(p)) for p in range(pages_per_step)]
    in_specs += [
        pl.BlockSpec((1, 16, IDX_DIM), lambda b, g, pt: (b, 0, 0)),
        pl.BlockSpec((1, IDX_DIM, PAGE_SIZE), lambda b, g, pt: (b, 0, 0)),
        pl.BlockSpec((1, 16, 1), lambda b, g, pt: (b, 0, 0)),
    ]
    grid_spec = pltpu.PrefetchScalarGridSpec(
        num_scalar_prefetch=1, grid=(n_seq, steps), in_specs=in_specs,
        out_specs=pl.BlockSpec((n_seq, rows, LANES), lambda b, g, pt: (0, 0, 0)),
        scratch_shapes=[pltpu.VMEM((n_seq, rows, LANES), jnp.int32)],
    )
    return pl.pallas_call(
        functools.partial(_decode_select_kernel, pages_per_step=pages_per_step, n_pages=n_pages, topk=topk),
        out_shape=jax.ShapeDtypeStruct((n_seq, rows, LANES), jnp.float32),
        grid_spec=grid_spec,
        compiler_params=pltpu.CompilerParams(dimension_semantics=("arbitrary", "arbitrary"),
                                             vmem_limit_bytes=VMEM_LIMIT),
        name="decode_select",
    )(page_table, *([cache_kidx] * pages_per_step), qi_rows, ki_new, wi_rows)


def _decode_attn_kernel(pt_ref, *refs, pages_per_step, n_pages):
    del pt_ref
    k_refs = refs[:pages_per_step]
    v_refs = refs[pages_per_step:2 * pages_per_step]
    q_ref, kn_ref, vn_ref, bias_ref, o_ref, m_ref, l_ref, acc_ref = refs[2 * pages_per_step:]
    g = pl.program_id(1)
    f32 = jnp.float32
    q_col = q_ref[0]
    q_full = jnp.broadcast_to(q_col, acc_ref.shape)

    @pl.when(g == 0)
    def _():
        m_ref[...] = jnp.full(m_ref.shape, NEG_BIAS, f32)
        l_ref[...] = jnp.zeros(l_ref.shape, f32)
        acc_ref[...] = jnp.zeros(acc_ref.shape, f32)

    for p in range(pages_per_step):
        bias = bias_ref[0, pl.ds(g * pages_per_step + p, 1), :]
        x = jnp.sum(k_refs[p][0] * q_full, axis=1, keepdims=True) + bias[None]
        m_old = m_ref[...]
        m_new = jnp.maximum(m_old, jnp.max(x, axis=2, keepdims=True))
        alpha = jnp.exp2(m_old - m_new)
        pe = jnp.exp2(x - m_new)
        l_ref[...] = alpha * l_ref[...] + jnp.sum(pe, axis=2, keepdims=True)
        m_ref[...] = m_new
        acc_ref[...] = acc_ref[...] * alpha + v_refs[p][0] * pe

    @pl.when(g == pl.num_programs(1) - 1)
    def _():
        x_new = jnp.sum(q_col * kn_ref[0], axis=1, keepdims=True) + bias_ref[0, n_pages:n_pages + 1, 0:1][None]
        m_old = m_ref[...]
        m_new = jnp.maximum(m_old, x_new)
        alpha = jnp.exp2(m_old - m_new)
        pe_new = jnp.exp2(x_new - m_new)
        denom = alpha * l_ref[...] + pe_new
        o_ref[0] = (jnp.sum(acc_ref[...], axis=2, keepdims=True) * alpha + pe_new * vn_ref[0]) / denom


def _decode_attention(page_table, cache_kt, cache_vt, q_cols, k_new, v_new, bias, *, pages_per_step):
    n_seq, n_pages = page_table.shape
    steps = n_pages // pages_per_step
    rows = _decode_rows(n_pages)

    def page_map(p):
        return lambda b, g, pt: (pt[b, g * pages_per_step + p], 0, 0, 0)

    page_specs = [pl.BlockSpec((1, N_HEADS, HEAD_DIM, PAGE_SIZE), page_map(p)) for p in range(pages_per_step)]
    col = pl.BlockSpec((1, N_HEADS, HEAD_DIM, 1), lambda b, g, pt: (b, 0, 0, 0))
    in_specs = page_specs + page_specs + [col, col, col, pl.BlockSpec((1, rows, LANES), lambda b, g, pt: (b, 0, 0))]
    grid_spec = pltpu.PrefetchScalarGridSpec(
        num_scalar_prefetch=1, grid=(n_seq, steps), in_specs=in_specs, out_specs=col,
        scratch_shapes=[pltpu.VMEM((N_HEADS, 1, 1), jnp.float32), pltpu.VMEM((N_HEADS, 1, 1), jnp.float32),
                        pltpu.VMEM((N_HEADS, HEAD_DIM, PAGE_SIZE), jnp.float32)],
    )
    return pl.pallas_call(
        functools.partial(_decode_attn_kernel, pages_per_step=pages_per_step, n_pages=n_pages),
        out_shape=jax.ShapeDtypeStruct((n_seq, N_HEADS, HEAD_DIM, 1), jnp.float32),
        grid_spec=grid_spec,
        compiler_params=pltpu.CompilerParams(dimension_semantics=("parallel", "arbitrary"),
                                             vmem_limit_bytes=VMEM_LIMIT),
        name="decode_attention",
    )(page_table, *([cache_kt] * pages_per_step), *([cache_vt] * pages_per_step), q_cols, k_new, v_new, bias)


def _shifted(u, halo, shift):
    rolled = pltpu.roll(u, shift, 0)
    row = lax.broadcasted_iota(jnp.int32, u.shape, 0)
    out = rolled
    for r in range(shift):
        out = jnp.where(row == r, halo[8 - shift + r:8 - shift + r + 1, :], out)
    return out


def _merge_kernel(*refs, sequential, tiles):
    if sequential:
        x_ref, a_ref, u_ref, cb_ref, sga_ref, sgc_ref, cw_ref, woa_ref, woc_ref, wm_ref, h_ref, halo_ref = refs
        u = u_ref[...]

        @pl.when(pl.program_id(0) % tiles == 0)
        def _():
            halo_ref[...] = jnp.zeros(halo_ref.shape, halo_ref.dtype)

        halo = halo_ref[...]
        u1 = _shifted(u, halo, 1)
        u2 = _shifted(u, halo, 2)
        halo_ref[...] = u[u.shape[0] - 8:, :]
    else:
        x_ref, a_ref, u_ref, cb_ref, sga_ref, sgc_ref, cw_ref, woa_ref, woc_ref, wm_ref, s0_ref, s1_ref, h_ref = refs
        u = u_ref[...]
        u2 = s0_ref[...]
        u1 = s1_ref[...]
    cw = cw_ref[...]
    conv = cb_ref[...] * (u2 * cw[0:1, :] + u1 * cw[1:2, :] + u * cw[2:3, :])
    mixed = sga_ref[...] * _dot(a_ref[...], woa_ref[...]) + sgc_ref[...] * _dot(_bf(conv), woc_ref[...])
    h_ref[...] = x_ref[...] + _dot(_bf(mixed), wm_ref[...])


def _merge(x, attn, u, cb, sga, sgc, conv_w, woa, woc, wm, states, *, tm, tiles):
    rows, d = x.shape
    sequential = states is None
    row = lambda i: (i, 0)
    const = lambda i: (0, 0)
    in_specs = [pl.BlockSpec((tm, d), row), pl.BlockSpec((tm, ATTN_DIM), row), pl.BlockSpec((tm, CONV_DIM), row),
                pl.BlockSpec((tm, CONV_DIM), row), pl.BlockSpec((tm, d), row), pl.BlockSpec((tm, d), row),
                pl.BlockSpec(conv_w.shape, const), pl.BlockSpec(woa.shape, const), pl.BlockSpec(woc.shape, const),
                pl.BlockSpec(wm.shape, const)]
    args = [x, attn, u, cb, sga, sgc, conv_w, woa, woc, wm]
    scratch = []
    if sequential:
        scratch = [pltpu.VMEM((8, CONV_DIM), jnp.float32)]
    else:
        in_specs += [pl.BlockSpec((tm, CONV_DIM), row), pl.BlockSpec((tm, CONV_DIM), row)]
        args += list(states)
    return pl.pallas_call(
        functools.partial(_merge_kernel, sequential=sequential, tiles=tiles),
        out_shape=jax.ShapeDtypeStruct((rows, d), jnp.float32),
        grid=(rows // tm,), in_specs=in_specs, out_specs=pl.BlockSpec((tm, d), row), scratch_shapes=scratch,
        compiler_params=pltpu.CompilerParams(dimension_semantics=("arbitrary",), vmem_limit_bytes=VMEM_LIMIT),
        name="merge",
    )(*args)


def _ffn_kernel(*refs, sequential, tiles, tail_off, tail_rows):
    if sequential:
        h_ref, nw_ref, wg_ref, wv_ref, wd_ref, cw_ref, cbias_ref, y_ref, gt_ref, hn_ref, halo_ref = refs
    else:
        h_ref, nw_ref, wg_ref, wv_ref, wd_ref, cw_ref, cbias_ref, s0_ref, s1_ref, y_ref, gt_ref, hn_ref = refs
    c = pl.program_id(1)

    @pl.when(c == 0)
    def _():
        hn_ref[...] = _bf(_rms(h_ref[...], nw_ref[...]))

    hn = hn_ref[...]
    g = _dot(hn, wg_ref[...])
    val = _dot(hn, wv_ref[...])
    gt_ref[0] = g[tail_off:tail_off + tail_rows, :]
    if sequential:
        @pl.when(pl.program_id(0) % tiles == 0)
        def _():
            halo_ref[c] = jnp.zeros(halo_ref.shape[1:], halo_ref.dtype)

        halo = halo_ref[c]
        g1 = _shifted(g, halo, 1)
        g2 = _shifted(g, halo, 2)
        halo_ref[c] = g[g.shape[0] - 8:, :]
    else:
        g2 = s0_ref[...]
        g1 = s1_ref[...]
    cw = cw_ref[...]
    gc = g2 * cw[0:1, :] + g1 * cw[1:2, :] + g * cw[2:3, :] + cbias_ref[...]
    act = gc * jax.nn.sigmoid(gc) * val
    part = _dot(_bf(act), wd_ref[...])

    @pl.when(c == 0)
    def _():
        y_ref[...] = h_ref[...] + part

    @pl.when(c != 0)
    def _():
        y_ref[...] = y_ref[...] + part


def _ffn(h, nw, wg, wv, wd, conv_w, conv_b, states, *, tm, tiles, n_chunk, tail_off, tail_rows):
    rows, d = h.shape
    dff = wg.shape[1]
    ch = dff // n_chunk
    sequential = states is None
    row = lambda i, c: (i, 0)
    in_specs = [pl.BlockSpec((tm, d), row), pl.BlockSpec((1, d), lambda i, c: (0, 0)),
                pl.BlockSpec((d, ch), lambda i, c: (0, c)), pl.BlockSpec((d, ch), lambda i, c: (0, c)),
                pl.BlockSpec((ch, d), lambda i, c: (c, 0)), pl.BlockSpec((conv_w.shape[0], ch), lambda i, c: (0, c)),
                pl.BlockSpec((1, ch), lambda i, c: (0, c))]
    args = [h, nw, wg, wv, wd, conv_w, conv_b]
    scratch = [pltpu.VMEM((tm, d), jnp.bfloat16)]
    if sequential:
        scratch.append(pltpu.VMEM((n_chunk, 8, ch), jnp.float32))
    else:
        in_specs += [pl.BlockSpec((tm, ch), lambda i, c: (i, c)), pl.BlockSpec((tm, ch), lambda i, c: (i, c))]
        args += list(states)
    n_tiles = rows // tm
    return pl.pallas_call(
        functools.partial(_ffn_kernel, sequential=sequential, tiles=tiles, tail_off=tail_off, tail_rows=tail_rows),
        out_shape=(jax.ShapeDtypeStruct((rows, d), jnp.float32),
                   jax.ShapeDtypeStruct((n_tiles, tail_rows, dff), jnp.float32)),
        grid=(n_tiles, n_chunk), in_specs=in_specs,
        out_specs=(pl.BlockSpec((tm, d), row), pl.BlockSpec((1, tail_rows, ch), lambda i, c: (i, 0, c))),
        scratch_shapes=scratch,
        compiler_params=pltpu.CompilerParams(dimension_semantics=("arbitrary", "arbitrary"),
                                             vmem_limit_bytes=VMEM_LIMIT),
        name="ffn",
    )(*args)


def _rope_tables(pos):
    half = HEAD_DIM // 2
    freqs = ROPE_THETA ** (-jnp.arange(half, dtype=jnp.float32) / half)
    ang = pos.astype(jnp.float32)[:, None] * freqs[None, :]
    cos, sin = jnp.cos(ang), jnp.sin(ang)
    reps = LANES // HEAD_DIM
    return jnp.concatenate([cos, cos] * reps, axis=1), jnp.concatenate([-sin, sin] * reps, axis=1)


def kernel(x_prompt, x_sample, cache_k, cache_v, cache_kidx, state_conv, state_ffn_conv, page_table, meta_tokens, attn_norm_w, w_in, q_norm_w, k_norm_w, conv_w, w_o_attn, w_o_conv, w_merge, ffn_norm_w, w_up, ffn_conv_w, ffn_conv_b, w_down):
    f32 = jnp.float32
    n_b, seq, d = x_prompt.shape
    n_db, n_ds = x_sample.shape[0], x_sample.shape[1]
    depth = w_in.shape[0]
    assert depth == 1 and n_ds == 1
    n_pages = page_table.shape[1]
    past = n_pages * PAGE_SIZE
    t_len = seq + N_META
    assert t_len % 8 == 0
    topk_p = min(TOPK_MAX, seq // 4)
    topk_s = min(TOPK_MAX, (past + n_ds) // 4)
    n_real = -(-t_len // Q_BLOCK)
    tm = KEY_CHUNK
    tiles = -(-t_len // tm)
    tp = tiles * tm
    dff = w_down.shape[1]
    n_chunk = 2 if dff % 256 == 0 else 1

    l = 0
    o_idx = 3 * ATTN_DIM
    n_idx = IDX_HEADS * IDX_DIM + IDX_DIM + IDX_HEADS
    o_conv = o_idx + n_idx
    wqkv = _bf(w_in[l][:, :o_idx])
    wvt = wqkv[:, 2 * ATTN_DIM:].T
    w_idx = jnp.pad(w_in[l][:, o_idx:o_conv], ((0, 0), (0, 3 * LANES - n_idx)))
    whi, wlo = _split_hi_lo(w_idx)
    w_b = _bf(w_in[l][:, o_conv:])
    woa, woc, wmg = _bf(w_o_attn[l]), _bf(w_o_conv[l]), _bf(w_merge[l])
    wg, wv, wd = _bf(w_up[l][:, :dff]), _bf(w_up[l][:, dff:]), _bf(w_down[l])
    nw = attn_norm_w[l][None, :]
    fnw = ffn_norm_w[l][None, :]
    qnw = jnp.tile(q_norm_w[l], N_HEADS)[None, :]
    knw = jnp.tile(k_norm_w[l], N_HEADS)[None, :]
    hid = np.arange(ATTN_DIM) // HEAD_DIM
    gmat = jnp.asarray(hid[:, None] == hid[None, :], jnp.bfloat16)
    fcb = ffn_conv_b[l][None, :]

    hp = jnp.concatenate([jnp.broadcast_to(meta_tokens.astype(f32)[None], (n_b, N_META, d)), x_prompt,
                          jnp.zeros((n_b, tp - t_len, d), f32)], axis=1).reshape(n_b * tp, d)
    cos_p, sin_p = _rope_tables(jnp.arange(tp, dtype=jnp.int32))
    q, kf, kb, vf, vt, qi3, kif, ki3, kiwt = _inproj_a(hp, nw, wqkv, wvt, whi, wlo, qnw, knw, gmat, cos_p, sin_p,
                                                        n_seq=n_b, tm=tm, emit_vt=True, t_out=t_len)
    u, cb, sga, sgc = _inproj_b(hp, nw, w_b, tm=tm)
    attn = _prompt_attention(q, kb, vt, qi3, ki3, kiwt, n_seq=n_b, topk=topk_p, n_real=n_real)
    h = _merge(hp, attn, u, cb, sga, sgc, conv_w[l], woa, woc, wmg, None, tm=tm, tiles=tiles)
    tail_tile, tail_off = (t_len - 8) // tm, (t_len - 8) % tm
    y, g_tail = _ffn(h, fnw, wg, wv, wd, ffn_conv_w[l], fcb, None, tm=tm, tiles=tiles, n_chunk=n_chunk,
                     tail_off=tail_off, tail_rows=8)

    y_prompt = y.reshape(n_b, tp, d)[:, N_META:t_len]
    new_k_p = kf.reshape(n_b, N_HEADS, HEAD_DIM, t_len).transpose(0, 3, 1, 2)[None]
    new_v_p = vf.reshape(n_b, N_HEADS, HEAD_DIM, t_len).transpose(0, 3, 1, 2)[None]
    new_ki_p = kif.transpose(0, 2, 1)[None]
    new_conv_p = u.reshape(n_b, tp, CONV_DIM)[:, t_len - 2:t_len][None]
    new_ffn_p = g_tail.reshape(n_b, tiles, 8, dff)[:, tail_tile, 6:8][None]

    xs = x_sample.reshape(n_db, d)
    n_pad = -(-n_db // LANES) * LANES
    xs_pad = jnp.pad(xs, ((0, n_pad - n_db), (0, 0)))
    cos_s, sin_s = _rope_tables(jnp.full((n_pad,), past, jnp.int32))
    qs, kfs, _, vfs, _, qi3s, kifs, _, kiwts = _inproj_a(xs_pad, nw, wqkv, wvt, whi, wlo, qnw, knw, gmat, cos_s, sin_s,
                                                          n_seq=1, tm=n_pad, emit_vt=False)
    qs, kfs, vfs, kifs = qs[:n_db], kfs[:n_db], vfs[:n_db], kifs[:n_db]
    us, cbs, sgas, sgcs = _inproj_b(xs, nw, w_b, tm=n_db)
    qi4 = qi3s[:n_db].astype(f32).reshape(n_db, IDX_HEADS, IDX_K // IDX_DIM, IDX_DIM)
    qi_rows = jnp.pad(qi4[:, :, 0] + qi4[:, :, 1], ((0, 0), (0, 16 - IDX_HEADS), (0, 0)))
    wi_rows = jnp.pad(kiwts[0, IDX_DIM:IDX_DIM + IDX_HEADS, :n_db].T, ((0, 0), (0, 16 - IDX_HEADS)))[:, :, None]
    ki_new = jnp.broadcast_to(kifs[:, :, None], (n_db, IDX_DIM, PAGE_SIZE))
    pps_sel = 16 if n_pages % 16 == 0 else 1
    pps_att = 8 if n_pages % 8 == 0 else 1
    kidx_t = jnp.transpose(cache_kidx[l], (0, 2, 1))
    cache_kt = jnp.transpose(cache_k[l], (0, 2, 3, 1))
    cache_vt = jnp.transpose(cache_v[l], (0, 2, 3, 1))
    bias = _decode_select(page_table, kidx_t, qi_rows, ki_new, wi_rows, topk=topk_s, pages_per_step=pps_sel)
    as_cols = lambda a: a.astype(f32).reshape(n_db, N_HEADS, HEAD_DIM, 1)
    attn_s = _decode_attention(page_table, cache_kt, cache_vt, as_cols(qs), as_cols(kfs), as_cols(vfs), bias,
                               pages_per_step=pps_att)
    sc = state_conv[l]
    hs = _merge(xs, _bf(attn_s.reshape(n_db, ATTN_DIM)), us, cbs, sgas, sgcs, conv_w[l], woa, woc, wmg,
                (sc[:, 0], sc[:, 1]), tm=n_db, tiles=1)
    sf = state_ffn_conv[l]
    ys, g_s = _ffn(hs, fnw, wg, wv, wd, ffn_conv_w[l], fcb, (sf[:, 0], sf[:, 1]), tm=n_db, tiles=1, n_chunk=n_chunk,
                   tail_off=0, tail_rows=n_db)

    y_sample = ys.reshape(n_db, n_ds, d)
    new_k_s = kfs.reshape(1, n_db, n_ds, N_HEADS, HEAD_DIM)
    new_v_s = vfs.reshape(1, n_db, n_ds, N_HEADS, HEAD_DIM)
    new_ki_s = kifs.reshape(1, n_db, n_ds, IDX_DIM)
    new_conv_s = jnp.stack([sc[:, 1], us], axis=1)[None]
    new_ffn_s = jnp.stack([sf[:, 1], g_s[0]], axis=1)[None]
    return (y_prompt, y_sample, new_k_p, new_v_p, new_ki_p, new_conv_p, new_ffn_p,
            new_k_s, new_v_s, new_ki_s, new_conv_s, new_ffn_s)
```
